```python
import jax, jax.numpy as jnp
from jax import lax
import numpy as np

D_MODEL = 1024
BATCH = 8
SEQ = 2048
DEPTH = 1
DEC_BATCH = 128
DEC_SEQ = 4
PAST_LEN = 8192
PAGE_SIZE = 128

EPS = 1e-6
N_MEM = 256
POOL_WINDOWS = (2, 4, 8, 16)
POOL_GROUP = 128
POOL_WIDTH = POOL_GROUP * len(POOL_WINDOWS)
POOL_HIST = max(POOL_WINDOWS) - 1
MLA_HEADS = 8
Q_LORA = 384
KV_LORA = 256
QK_NOPE = 64
QK_ROPE = 32
V_HEAD = 64
ROPE_BASE = 10000.0
ATTN_BLOCK = 128
X_HEADS = 4
X_HEAD_DIM = 128
X_WIDTH = X_HEADS * X_HEAD_DIM
N_BRANCH = 3
SPLITS = (POOL_WIDTH, Q_LORA, KV_LORA, QK_ROPE, X_WIDTH)
IN_WIDTH = sum(SPLITS) + N_BRANCH * D_MODEL
PEER_HEADS = 8
N_KEYS = 128
N_EXPERTS = N_KEYS * N_KEYS
PEER_QDIM = 256
PEER_HALF = PEER_QDIM // 2
PEER_TOPK = 16
PEER_CHUNK = 128

kernel_name = 'pool_mla_memory_peer_hybrid_step'


def rmsnorm(x, g):
    xf = x.astype(jnp.float32)
    y = xf * lax.rsqrt(jnp.mean(xf * xf, axis=-1, keepdims=True) + EPS)
    return (y * g.astype(jnp.float32)).astype(x.dtype)


def rope(x, start):
    T = x.shape[1]
    half = QK_ROPE // 2
    inv = jnp.power(ROPE_BASE, -jnp.arange(half, dtype=jnp.float32) / half)
    pos = (start + jnp.arange(T, dtype=jnp.int32)).astype(jnp.float32)
    ang = pos[:, None] * inv[None, :]
    shape = (T,) + (1,) * (x.ndim - 3) + (half,)
    cos = jnp.cos(ang).reshape(shape)
    sin = jnp.sin(ang).reshape(shape)
    xf = x.astype(jnp.float32)
    x1, x2 = xf[..., :half], xf[..., half:]
    return jnp.concatenate([x1 * cos - x2 * sin, x1 * sin + x2 * cos], axis=-1).astype(x.dtype)


def split_projection(xn, w_in):
    z = xn @ w_in
    offs = [int(o) for o in np.cumsum(SPLITS)]
    u, cq, ckv, kr, qx, gates = jnp.split(z, offs, axis=-1)
    return u, cq, ckv, kr, qx, gates


def pool_mix(u, hist, start, w_pool_grp, pool_scale):
    B, T, _ = u.shape
    zc = jnp.concatenate([hist, u], axis=1)
    z = zc.astype(jnp.float32)
    cs = jnp.concatenate([jnp.zeros_like(z[:, :1]), jnp.cumsum(z, axis=1)], axis=1)
    pos = start + jnp.arange(T, dtype=jnp.int32)
    end = cs[:, POOL_HIST + 1:]
    cur = z[:, POOL_HIST:]
    groups = []
    for g, w in enumerate(POOL_WINDOWS):
        sl = slice(g * POOL_GROUP, (g + 1) * POOL_GROUP)
        wsum = end[..., sl] - cs[:, POOL_HIST + 1 - w:POOL_HIST + 1 - w + T, sl]
        cnt = jnp.minimum(pos + 1, w).astype(jnp.float32)[None, :, None]
        groups.append(wsum / cnt - cur[..., sl])
    d = jnp.stack(groups, axis=2)
    y = jnp.einsum('btgc,gcd->btgd', d, w_pool_grp.astype(jnp.float32)).reshape(B, T, POOL_WIDTH)
    y = y * pool_scale.astype(jnp.float32)
    return y.astype(u.dtype), zc[:, -POOL_HIST:]


def mla_qkv(cq, ckv, kr, q_norm, kv_norm, w_q_up, start):
    q = jnp.einsum('btc,chd->bthd', rmsnorm(cq, q_norm), w_q_up)
    q_nope = q[..., :QK_NOPE]
    q_rope = rope(q[..., QK_NOPE:], start)
    c_kv = rmsnorm(ckv, kv_norm)
    k_rope = rope(kr, start)
    return q_nope, q_rope, c_kv, k_rope


def mla_prompt_attention(q_nope, q_rope, c_kv, k_rope, w_uk, w_uv):
    B, S = c_kv.shape[:2]
    k_nope = jnp.einsum('bsc,chd->bshd', c_kv, w_uk)
    v = jnp.einsum('bsc,chd->bshd', c_kv, w_uv)
    scale = (QK_NOPE + QK_ROPE) ** -0.5
    outs = []
    for i in range(S // ATTN_BLOCK):
        q0 = i * ATTN_BLOCK
        L = q0 + ATTN_BLOCK
        s = (jnp.einsum('bqhd,bkhd->bhqk', q_nope[:, q0:L], k_nope[:, :L])
             + jnp.einsum('bqhr,bkr->bhqk', q_rope[:, q0:L], k_rope[:, :L])).astype(jnp.float32) * scale
        qpos = q0 + jnp.arange(ATTN_BLOCK)
        kpos = jnp.arange(L)
        s = jnp.where(kpos[None, :] <= qpos[:, None], s, -jnp.inf)
        p = jax.nn.softmax(s, axis=-1).astype(v.dtype)
        outs.append(jnp.einsum('bhqk,bkhd->bqhd', p, v[:, :L]))
    return jnp.concatenate(outs, axis=1)


def mla_sample_attention(q_nope, q_rope, c_kv, k_rope, lat_past, kr_past, w_uk, w_uv):
    T = c_kv.shape[1]
    P = lat_past.shape[1]
    scale = (QK_NOPE + QK_ROPE) ** -0.5
    q_lat = jnp.einsum('bthd,chd->bthc', q_nope, w_uk)
    s_past = (jnp.einsum('bthc,bkc->bhtk', q_lat, lat_past)
              + jnp.einsum('bthr,bkr->bhtk', q_rope, kr_past)).astype(jnp.float32) * scale
    s_new = (jnp.einsum('bthc,bkc->bhtk', q_lat, c_kv)
             + jnp.einsum('bthr,bkr->bhtk', q_rope, k_rope)).astype(jnp.float32) * scale
    causal = jnp.arange(T)[None, :] <= jnp.arange(T)[:, None]
    s_new = jnp.where(causal, s_new, -jnp.inf)
    p = jax.nn.softmax(jnp.concatenate([s_past, s_new], axis=-1), axis=-1).astype(c_kv.dtype)
    o_lat = (jnp.einsum('bhtk,bkc->bthc', p[..., :P], lat_past)
             + jnp.einsum('bhtk,bkc->bthc', p[..., P:], c_kv))
    return jnp.einsum('bthc,chd->bthd', o_lat, w_uv)


def memory_kv(mem, mem_norm, w_mem_k, w_mem_v):
    mn = rmsnorm(mem, mem_norm)
    return (jnp.einsum('bmd,dhe->bmhe', mn, w_mem_k),
            jnp.einsum('bmd,dhe->bmhe', mn, w_mem_v))


def memory_attention(qx, mem_k, mem_v):
    B, T, _ = qx.shape
    q = qx.reshape(B, T, X_HEADS, X_HEAD_DIM)
    s = jnp.einsum('bthd,bmhd->bhtm', q, mem_k).astype(jnp.float32) * (X_HEAD_DIM ** -0.5)
    p = jax.nn.softmax(s, axis=-1).astype(mem_v.dtype)
    return jnp.einsum('bhtm,bmhd->bthd', p, mem_v).reshape(B, T, X_WIDTH)


def merge_branches(pool_y, mla_o, mem_o, gates, lp):
    B, T, _ = gates.shape
    g = jax.nn.sigmoid(gates.astype(jnp.float32)).reshape(B, T, N_BRANCH, D_MODEL).astype(pool_y.dtype)
    merged = (g[:, :, 0] * (pool_y @ lp['w_pool_out'])
              + g[:, :, 1] * (mla_o.reshape(B, T, MLA_HEADS * V_HEAD) @ lp['w_mla_out'])
              + g[:, :, 2] * (mem_o @ lp['w_x_out']))
    return merged @ lp['w_o']


def peer_ffn(h, lp):
    B, T, D = h.shape
    N = B * T
    hf = h.reshape(N, D)
    q = jnp.einsum('nd,dhc->nhc', hf, lp['w_peer_q'])
    s1 = jnp.einsum('nhc,kc->nhk', q[..., :PEER_HALF], lp['peer_k1']).astype(jnp.float32)
    s2 = jnp.einsum('nhc,kc->nhk', q[..., PEER_HALF:], lp['peer_k2']).astype(jnp.float32)
    v1, i1 = lax.top_k(s1, PEER_TOPK)
    v2, i2 = lax.top_k(s2, PEER_TOPK)
    cand = (v1[..., :, None] + v2[..., None, :]).reshape(N, PEER_HEADS, PEER_TOPK * PEER_TOPK)
    cidx = (i1[..., :, None] * N_KEYS + i2[..., None, :]).reshape(N, PEER_HEADS, PEER_TOPK * PEER_TOPK)
    sc, sel = lax.top_k(cand, PEER_TOPK)
    idx = jnp.take_along_axis(cidx, sel, axis=-1).reshape(N, PEER_HEADS * PEER_TOPK)
    gate = jax.nn.softmax(sc, axis=-1).reshape(N, PEER_HEADS * PEER_TOPK).astype(h.dtype)
    n_pad = (-N) % PEER_CHUNK
    nc = (N + n_pad) // PEER_CHUNK
    hp = jnp.pad(hf, ((0, n_pad), (0, 0))).reshape(nc, PEER_CHUNK, D)
    ip = jnp.pad(idx, ((0, n_pad), (0, 0))).reshape(nc, PEER_CHUNK, PEER_HEADS * PEER_TOPK)
    gp = jnp.pad(gate, ((0, n_pad), (0, 0))).reshape(nc, PEER_CHUNK, PEER_HEADS * PEER_TOPK)
    peer_u = lp['peer_u']
    peer_v = lp['peer_v']

    def chunk(args):
        hc, ic, gc = args
        a = jax.nn.gelu(jnp.einsum('cd,ckd->ck', hc, peer_u[ic]), approximate=False)
        return jnp.einsum('ck,ckd->cd', gc * a, peer_v[ic])

    out = lax.map(chunk, (hp, ip, gp))
    return out.reshape(nc * PEER_CHUNK, D)[:N].reshape(B, T, D)


def layer_prompt(x, mem, lp):
    xn = rmsnorm(x, lp['attn_norm'])
    u, cq, ckv, kr, qx, gates = split_projection(xn, lp['w_in'])
    hist0 = jnp.zeros((x.shape[0], POOL_HIST, POOL_WIDTH), u.dtype)
    pool_y, new_hist = pool_mix(u, hist0, 0, lp['w_pool_grp'], lp['pool_scale'])
    q_nope, q_rope, c_kv, k_rope = mla_qkv(cq, ckv, kr, lp['q_norm'], lp['kv_norm'], lp['w_q_up'], 0)
    mla_o = mla_prompt_attention(q_nope, q_rope, c_kv, k_rope, lp['w_uk'], lp['w_uv'])
    mem_k, mem_v = memory_kv(mem, lp['mem_norm'], lp['w_mem_k'], lp['w_mem_v'])
    mem_o = memory_attention(qx, mem_k, mem_v)
    h = x + merge_branches(pool_y, mla_o, mem_o, gates, lp)
    y = h + peer_ffn(rmsnorm(h, lp['ffn_norm']), lp)
    return y, c_kv, k_rope, new_hist, mem_k, mem_v


def layer_sample(x, lat_past, kr_past, hist, mem_k, mem_v, lp):
    xn = rmsnorm(x, lp['attn_norm'])
    u, cq, ckv, kr, qx, gates = split_projection(xn, lp['w_in'])
    pool_y, new_hist = pool_mix(u, hist, PAST_LEN, lp['w_pool_grp'], lp['pool_scale'])
    q_nope, q_rope, c_kv, k_rope = mla_qkv(cq, ckv, kr, lp['q_norm'], lp['kv_norm'], lp['w_q_up'], PAST_LEN)
    mla_o = mla_sample_attention(q_nope, q_rope, c_kv, k_rope, lat_past, kr_past, lp['w_uk'], lp['w_uv'])
    mem_o = memory_attention(qx, mem_k, mem_v)
    h = x + merge_branches(pool_y, mla_o, mem_o, gates, lp)
    y = h + peer_ffn(rmsnorm(h, lp['ffn_norm']), lp)
    return y, c_kv, k_rope, new_hist


def setup_inputs(seed: int = 0) -> dict:
    key = jax.random.key(seed)
    ks = iter(jax.random.split(key, 40))
    f32 = jnp.float32

    def nrm(shape, scale=1.0):
        return jax.random.normal(next(ks), shape, f32) * scale

    def gain(shape):
        return 1.0 + 0.02 * jax.random.normal(next(ks), shape, f32)

    n_pages = PAST_LEN // PAGE_SIZE
    n_used = DEC_BATCH * n_pages
    n_phys = n_used + n_used // 4
    x_prompt = nrm((BATCH, SEQ, D_MODEL))
    x_sample = nrm((DEC_BATCH, DEC_SEQ, D_MODEL))
    cache_kv_latent = nrm((DEPTH, n_phys, PAGE_SIZE, KV_LORA))
    cache_k_rope = nrm((DEPTH, n_phys, PAGE_SIZE, QK_ROPE))
    state_pool = nrm((DEPTH, DEC_BATCH, POOL_HIST, POOL_WIDTH))
    cache_mem_k = nrm((DEPTH, DEC_BATCH, N_MEM, X_HEADS, X_HEAD_DIM))
    cache_mem_v = nrm((DEPTH, DEC_BATCH, N_MEM, X_HEADS, X_HEAD_DIM))
    perm = jax.random.permutation(next(ks), n_phys)
    page_table = perm[:n_used].reshape(DEC_BATCH, n_pages).astype(jnp.int32)
    mem_prompt = nrm((BATCH, N_MEM, D_MODEL))
    return {
        'x_prompt': x_prompt,
        'x_sample': x_sample,
        'cache_kv_latent': cache_kv_latent,
        'cache_k_rope': cache_k_rope,
        'state_pool': state_pool,
        'cache_mem_k': cache_mem_k,
        'cache_mem_v': cache_mem_v,
        'page_table': page_table,
        'mem_prompt': mem_prompt,
        'attn_norm': gain((DEPTH, D_MODEL)),
        'w_in': nrm((DEPTH, D_MODEL, IN_WIDTH), D_MODEL ** -0.5),
        'q_norm': gain((DEPTH, Q_LORA)),
        'kv_norm': gain((DEPTH, KV_LORA)),
        'w_q_up': nrm((DEPTH, Q_LORA, MLA_HEADS, QK_NOPE + QK_ROPE), Q_LORA ** -0.5),
        'w_uk': nrm((DEPTH, KV_LORA, MLA_HEADS, QK_NOPE), KV_LORA ** -0.5),
        'w_uv': nrm((DEPTH, KV_LORA, MLA_HEADS, V_HEAD), KV_LORA ** -0.5),
        'w_pool_grp': nrm((DEPTH, len(POOL_WINDOWS), POOL_GROUP, POOL_GROUP), POOL_GROUP ** -0.5),
        'pool_scale': gain((DEPTH, POOL_WIDTH)),
        'mem_norm': gain((DEPTH, D_MODEL)),
        'w_mem_k': nrm((DEPTH, D_MODEL, X_HEADS, X_HEAD_DIM), D_MODEL ** -0.5),
        'w_mem_v': nrm((DEPTH, D_MODEL, X_HEADS, X_HEAD_DIM), D_MODEL ** -0.5),
        'w_pool_out': nrm((DEPTH, POOL_WIDTH, D_MODEL), POOL_WIDTH ** -0.5),
        'w_mla_out': nrm((DEPTH, MLA_HEADS * V_HEAD, D_MODEL), (MLA_HEADS * V_HEAD) ** -0.5),
        'w_x_out': nrm((DEPTH, X_WIDTH, D_MODEL), X_WIDTH ** -0.5),
        'w_o': nrm((DEPTH, D_MODEL, D_MODEL), D_MODEL ** -0.5),
        'ffn_norm': gain((DEPTH, D_MODEL)),
        'w_peer_q': nrm((DEPTH, D_MODEL, PEER_HEADS, PEER_QDIM), D_MODEL ** -0.5),
        'peer_k1': nrm((DEPTH, N_KEYS, PEER_HALF), PEER_HALF ** -0.5),
        'peer_k2': nrm((DEPTH, N_KEYS, PEER_HALF), PEER_HALF ** -0.5),
        'peer_u': nrm((DEPTH, N_EXPERTS, D_MODEL), D_MODEL ** -0.5),
        'peer_v': nrm((DEPTH, N_EXPERTS, D_MODEL), 0.25),
        'final_norm': gain((D_MODEL,)),
    }


def reference(x_prompt, x_sample, cache_kv_latent, cache_k_rope, state_pool, cache_mem_k, cache_mem_v,
              page_table, mem_prompt, attn_norm, w_in, q_norm, kv_norm, w_q_up, w_uk, w_uv, w_pool_grp,
              pool_scale, mem_norm, w_mem_k, w_mem_v, w_pool_out, w_mla_out, w_x_out, w_o, ffn_norm,
              w_peer_q, peer_k1, peer_k2, peer_u, peer_v, final_norm):
    n_pages = PAST_LEN // PAGE_SIZE
    yp, ys = x_prompt, x_sample
    lat_p, kr_p, pool_p, mk_p, mv_p = [], [], [], [], []
    lat_s, kr_s, pool_s = [], [], []
    for l in range(DEPTH):
        lp = dict(attn_norm=attn_norm[l], w_in=w_in[l], q_norm=q_norm[l], kv_norm=kv_norm[l],
                  w_q_up=w_q_up[l], w_uk=w_uk[l], w_uv=w_uv[l], w_pool_grp=w_pool_grp[l],
                  pool_scale=pool_scale[l], mem_norm=mem_norm[l], w_mem_k=w_mem_k[l], w_mem_v=w_mem_v[l],
                  w_pool_out=w_pool_out[l], w_mla_out=w_mla_out[l], w_x_out=w_x_out[l], w_o=w_o[l],
                  ffn_norm=ffn_norm[l], w_peer_q=w_peer_q[l], peer_k1=peer_k1[l], peer_k2=peer_k2[l],
                  peer_u=peer_u[l], peer_v=peer_v[l])
        yp, c_kv, k_rope, hist, mem_k, mem_v = layer_prompt(yp, mem_prompt, lp)
        lat_p.append(c_kv); kr_p.append(k_rope); pool_p.append(hist); mk_p.append(mem_k); mv_p.append(mem_v)
        lat_past = cache_kv_latent[l][page_table].reshape(DEC_BATCH, n_pages * PAGE_SIZE, KV_LORA)
        kr_past = cache_k_rope[l][page_table].reshape(DEC_BATCH, n_pages * PAGE_SIZE, QK_ROPE)
        ys, c_kv_s, k_rope_s, hist_s = layer_sample(ys, lat_past, kr_past, state_pool[l],
                                                    cache_mem_k[l], cache_mem_v[l], lp)
        lat_s.append(c_kv_s); kr_s.append(k_rope_s); pool_s.append(hist_s)
    y_prompt = rmsnorm(yp, final_norm)
    y_sample = rmsnorm(ys, final_norm)
    return (y_prompt, y_sample, jnp.stack(lat_p), jnp.stack(kr_p), jnp.stack(pool_p), jnp.stack(mk_p),
            jnp.stack(mv_p), jnp.stack(lat_s), jnp.stack(kr_s), jnp.stack(pool_s))
```

```python
import functools

import jax
import jax.numpy as jnp
import numpy as np
from jax import lax
from jax.experimental import pallas as pl
from jax.experimental.pallas import tpu as pltpu

F32 = jnp.float32
BF16 = jnp.bfloat16

EPS = 1e-6
POOL_WINDOWS = (2, 4, 8, 16)
POOL_GROUP = 128
POOL_HIST = max(POOL_WINDOWS) - 1
MLA_HEADS = 8
Q_LORA = 384
KV_LORA = 256
QK_NOPE = 64
QK_ROPE = 32
V_HEAD = 64
ROPE_BASE = 10000.0
X_HEADS = 4
X_HEAD_DIM = 128
PEER_HEADS = 8
N_KEYS = 128
PEER_HALF = 128
PEER_TOPK = 16

LANES = 128
HEAD_PAD = 128

NT_DIMS = (((1,), (1,)), ((), ()))

NEG_INF = float("-inf")


def _rms(x, g):
    return x * lax.rsqrt(jnp.mean(x * x, axis=-1, keepdims=True) + EPS) * g


def _dot(a, b):
    return jnp.dot(a, b, preferred_element_type=F32)


def _dot_nt(a, b):
    return lax.dot_general(a, b, NT_DIMS, preferred_element_type=F32)


def _resident(shape):
    nd = len(shape)
    return pl.BlockSpec(shape, lambda *_: (0,) * nd, pipeline_mode=pl.Buffered(1))


_C_U = (0, 512)
_C_CQ = (512, 896)
_C_CKV = (896, 1152)
_C_KRA = (1152, 1280)
_C_KRB = (1280, 1408)
_C_QX = (1408, 1920)
_C_G = (1920, 4992)


def _proj_kernel(sample, x_ref, cos_ref, sin_ref, an_ref, win_ref, qn_ref, kvn_ref, wq_ref, wx_ref,
                 u_ref, ckv_ref, kr_ref, qx_ref, g_ref, *outs):
    xn = _rms(x_ref[...], an_ref[...]).astype(BF16)

    def seg(c):
        return _dot(xn, win_ref[:, c[0]:c[1]])

    u_ref[...] = seg(_C_U)
    qx_ref[...] = seg(_C_QX)
    g_ref[...] = jax.nn.sigmoid(seg(_C_G))
    cos = cos_ref[...]
    sin = sin_ref[...]
    kr = seg(_C_KRA) * cos + seg(_C_KRB) * sin
    kr_ref[...] = kr
    c_kv = _rms(seg(_C_CKV), kvn_ref[...])
    ckv_ref[...] = c_kv
    cqn = _rms(seg(_C_CQ), qn_ref[...]).astype(BF16)
    nq = MLA_HEADS * HEAD_PAD
    qa = _dot(cqn, wq_ref[:, 0:nq])
    qb = _dot(cqn, wq_ref[:, nq:2 * nq])
    if sample:
        (qs_ref,) = outs
        for h in range(MLA_HEADS):
            sl = slice(h * HEAD_PAD, (h + 1) * HEAD_PAD)
            qh = (qa[:, sl] * cos + qb[:, sl] * sin).astype(BF16)
            qs_ref[:, h * 384:(h + 1) * 384] = _dot(qh, wx_ref[h])
    else:
        q_ref, k_ref, v_ref = outs
        kv = _dot(c_kv.astype(BF16), wx_ref[...])
        for h in range(MLA_HEADS):
            sl = slice(h * HEAD_PAD, (h + 1) * HEAD_PAD)
            q_ref[:, sl] = (qa[:, sl] * cos + qb[:, sl] * sin).astype(BF16)
            k_ref[:, sl] = (kv[:, sl] + kr).astype(BF16)
        v_ref[...] = kv[:, nq:2 * nq].astype(BF16)


def _proj_in(x, cos, sin, n_pos_tiles, w, sample):
    n, d = x.shape
    tm = min(512, n)
    nq = MLA_HEADS * HEAD_PAD
    row = lambda width: pl.BlockSpec((tm, width), lambda i: (i, 0))
    pos = pl.BlockSpec((tm, LANES), lambda i: (i % n_pos_tiles, 0))
    wx = w["w_abs"] if sample else w["w_kv"]
    in_specs = [row(d), pos, pos, _resident((1, d)), _resident(w["w_in"].shape), _resident((1, Q_LORA)),
                _resident((1, KV_LORA)), _resident(w["w_q"].shape), _resident(wx.shape)]
    out_shape = [jax.ShapeDtypeStruct((n, 512), F32), jax.ShapeDtypeStruct((n, KV_LORA), F32),
                 jax.ShapeDtypeStruct((n, LANES), F32), jax.ShapeDtypeStruct((n, 512), F32),
                 jax.ShapeDtypeStruct((n, 3 * d), F32)]
    out_specs = [row(512), row(KV_LORA), row(LANES), row(512), row(3 * d)]
    if sample:
        out_shape.append(jax.ShapeDtypeStruct((n, MLA_HEADS * 384), F32))
        out_specs.append(row(MLA_HEADS * 384))
    else:
        out_shape += [jax.ShapeDtypeStruct((n, nq), BF16)] * 3
        out_specs += [row(nq)] * 3
    return pl.pallas_call(
        functools.partial(_proj_kernel, sample),
        grid=(n // tm,),
        in_specs=in_specs,
        out_specs=out_specs,
        out_shape=out_shape,
        compiler_params=pltpu.CompilerParams(dimension_semantics=("arbitrary",)),
        name="proj_in_s" if sample else "proj_in_p",
    )(x, cos, sin, w["attn_norm"], w["w_in"], w["q_norm"], w["kv_norm"], w["w_q"], wx)


def _pool_kernel(start, u_ref, hist_ref, wg_ref, sc_ref, o_ref, ext_ref):
    j = pl.program_id(1)
    tm = u_ref.shape[1]
    base = POOL_HIST + 1

    @pl.when(j == 0)
    def _():
        ext_ref[0:1, :] = jnp.zeros((1, ext_ref.shape[1]), F32)
        ext_ref[1:base, :] = hist_ref[0]

    @pl.when(j > 0)
    def _():
        ext_ref[0:base, :] = ext_ref[tm:tm + base, :]

    ext_ref[base:base + tm, :] = u_ref[0]
    pos = start + j * tm + lax.broadcasted_iota(jnp.int32, (tm, 1), 0)
    for g, w in enumerate(POOL_WINDOWS):
        sl = slice(g * POOL_GROUP, (g + 1) * POOL_GROUP)
        cur = ext_ref[base:base + tm, sl]
        acc = cur
        for jj in range(1, w):
            acc = acc + ext_ref[base - jj:base - jj + tm, sl]
        cnt = jnp.minimum(pos + 1, w).astype(F32)
        dlt = acc / cnt - cur
        o_ref[0, :, sl] = _dot(dlt.astype(BF16), wg_ref[g]) * sc_ref[:, sl]


def _pool_mix(u, hist, start, w):
    b, t, width = u.shape
    tm = min(512, t)
    return pl.pallas_call(
        functools.partial(_pool_kernel, start),
        grid=(b, t // tm),
        in_specs=[pl.BlockSpec((1, tm, width), lambda i, j: (i, j, 0)),
                  pl.BlockSpec((1, POOL_HIST, width), lambda i, j: (i, 0, 0)),
                  _resident(w["w_pool_grp"].shape), _resident((1, width))],
        out_specs=pl.BlockSpec((1, tm, width), lambda i, j: (i, j, 0)),
        out_shape=jax.ShapeDtypeStruct((b, t, width), F32),
        scratch_shapes=[pltpu.VMEM((POOL_HIST + 1 + tm, width), F32)],
        compiler_params=pltpu.CompilerParams(dimension_semantics=("arbitrary", "arbitrary")),
        name="pool_mix",
    )(u, hist, w["w_pool_grp"], w["pool_scale"])


def _attn_kernel(tk, scale, q_ref, k_ref, v_ref, o_ref, m_ref, l_ref, acc_ref):
    qi = pl.program_id(1)
    tq = q_ref.shape[1]
    n_kv = ((qi + 1) * tq + tk - 1) // tk
    row = qi * tq + lax.broadcasted_iota(jnp.int32, (tq, tk), 0)
    col = lax.broadcasted_iota(jnp.int32, (tq, tk), 1)
    for h in range(MLA_HEADS):
        sl = slice(h * HEAD_PAD, (h + 1) * HEAD_PAD)
        q = q_ref[0, :, sl]
        m_ref[...] = jnp.full(m_ref.shape, NEG_INF, F32)
        l_ref[...] = jnp.zeros(l_ref.shape, F32)
        acc_ref[...] = jnp.zeros(acc_ref.shape, F32)

        def body(j, carry):
            k0 = pl.multiple_of(j * tk, tk)
            kb = k_ref[0, pl.ds(k0, tk), sl]
            vb = v_ref[0, pl.ds(k0, tk), sl]
            s = _dot_nt(q, kb) * scale
            s = jnp.where(col + k0 <= row, s, NEG_INF)
            m_old = m_ref[...]
            m_new = jnp.maximum(m_old, jnp.max(s, axis=-1, keepdims=True))
            alpha = jnp.exp(m_old - m_new)
            p = jnp.exp(s - m_new[:, 0:1])
            l_ref[...] = alpha * l_ref[...] + jnp.sum(p, axis=-1, keepdims=True)
            acc_ref[...] = alpha * acc_ref[...] + _dot(p.astype(BF16), vb)
            m_ref[...] = m_new
            return carry

        lax.fori_loop(0, n_kv, body, 0)
        o_ref[0, :, sl] = (acc_ref[...] / l_ref[...]).astype(o_ref.dtype)


def _prompt_attention(q, k, v):
    b, s, width = q.shape
    tq, tk = 256, 512
    scale = float((QK_NOPE + QK_ROPE) ** -0.5)
    return pl.pallas_call(
        functools.partial(_attn_kernel, tk, scale),
        grid=(b, s // tq),
        in_specs=[pl.BlockSpec((1, tq, width), lambda i, j: (i, j, 0)),
                  pl.BlockSpec((1, s, width), lambda i, j: (i, 0, 0)),
                  pl.BlockSpec((1, s, width), lambda i, j: (i, 0, 0))],
        out_specs=pl.BlockSpec((1, tq, width), lambda i, j: (i, j, 0)),
        out_shape=jax.ShapeDtypeStruct((b, s, width), BF16),
        scratch_shapes=[pltpu.VMEM((tq, HEAD_PAD), F32)] * 3,
        compiler_params=pltpu.CompilerParams(dimension_semantics=("arbitrary", "arbitrary")),
        name="attn_prompt",
    )(q, k, v)


_PAGES_PER_CHUNK = 8


def _sattn_kernel(n_pages, scale, pt_ref, q_ref, ckv_ref, krn_ref, wuv_ref, lat_hbm, kr_hbm,
                  o_ref, latbuf, krbuf, sem, s_scr, lat16):
    b = pl.program_id(0)
    nb = pl.num_programs(0)
    slot = b % 2
    page = latbuf.shape[2]

    def page_copies(bb, sl):
        cps = []
        for p in range(n_pages):
            pg = pt_ref[bb, p]
            cps.append(pltpu.make_async_copy(lat_hbm.at[pg], latbuf.at[sl, p], sem.at[0, sl]))
            cps.append(pltpu.make_async_copy(kr_hbm.at[pg], krbuf.at[sl, p], sem.at[1, sl]))
        return cps

    @pl.when(b == 0)
    def _():
        for cp in page_copies(0, 0):
            cp.start()

    @pl.when(b + 1 < nb)
    def _():
        for cp in page_copies(b + 1, 1 - slot):
            cp.start()

    for cp in page_copies(b, slot):
        cp.wait()

    q = q_ref[0]
    ql = q[:, 0:KV_LORA].astype(BF16)
    qr = q[:, KV_LORA:KV_LORA + QK_ROPE].astype(BF16)
    ck = _PAGES_PER_CHUNK * page
    for c in range(n_pages // _PAGES_PER_CHUNK):
        ps = slice(c * _PAGES_PER_CHUNK, (c + 1) * _PAGES_PER_CHUNK)
        lb = latbuf[slot, ps].reshape(ck, KV_LORA).astype(BF16)
        kb = krbuf[slot, ps].reshape(ck, QK_ROPE).astype(BF16)
        lat16[c * ck:(c + 1) * ck, :] = lb
        s_scr[:, c * ck:(c + 1) * ck] = (_dot_nt(ql, lb) + _dot_nt(qr, kb)) * scale
    cn = ckv_ref[0].astype(BF16)
    kn = krn_ref[0].astype(BF16)
    rows = q.shape[0]
    t_new = cn.shape[0]
    sn = (_dot_nt(ql, cn) + _dot_nt(qr, kn)) * scale
    assert t_new & (t_new - 1) == 0 and V_HEAD & (V_HEAD - 1) == 0
    t_of_row = lax.broadcasted_iota(jnp.int32, (rows, t_new), 0) & (t_new - 1)
    key = lax.broadcasted_iota(jnp.int32, (rows, t_new), 1)
    sn = jnp.where(key <= t_of_row, sn, NEG_INF)
    s = s_scr[...]
    m = jnp.maximum(jnp.max(s, axis=-1, keepdims=True), jnp.max(sn, axis=-1, keepdims=True))
    p = jnp.exp(s - m)
    pn = jnp.exp(sn - m)
    inv = 1.0 / (jnp.sum(p, axis=-1, keepdims=True) + jnp.sum(pn, axis=-1, keepdims=True))
    o = _dot((p * inv).astype(BF16), lat16[...])
    pnb = (pn * inv).astype(BF16).astype(F32)
    cnf = cn.astype(F32)
    for t in range(t_new):
        o = o + pnb[:, t:t + 1] * cnf[t:t + 1, :]
    full = _dot(o.astype(BF16), wuv_ref[...])
    lane_head = lax.shift_right_logical(lax.broadcasted_iota(jnp.int32, (t_new, full.shape[1]), 1),
                                        int(np.log2(V_HEAD)))
    out = jnp.zeros((t_new, full.shape[1]), F32)
    for h in range(MLA_HEADS):
        out = out + jnp.where(lane_head == h, full[h * t_new:(h + 1) * t_new, :], 0.0)
    o_ref[0] = out.astype(o_ref.dtype)


def _sample_attention(q_s, ckv8, kr8, page_table, cache_lat, cache_kr, w):
    b, rows, qw = q_s.shape
    n_pages = page_table.shape[1]
    page = cache_lat.shape[1]
    past = n_pages * page
    scale = float((QK_NOPE + QK_ROPE) ** -0.5)
    t8 = ckv8.shape[1]
    grid_spec = pltpu.PrefetchScalarGridSpec(
        num_scalar_prefetch=1,
        grid=(b,),
        in_specs=[pl.BlockSpec((1, rows, qw), lambda i, pt: (i, 0, 0)),
                  pl.BlockSpec((1, t8, KV_LORA), lambda i, pt: (i, 0, 0)),
                  pl.BlockSpec((1, t8, QK_ROPE), lambda i, pt: (i, 0, 0)),
                  pl.BlockSpec(w["w_uv_flat"].shape, lambda i, pt: (0, 0)),
                  pl.BlockSpec(memory_space=pl.ANY),
                  pl.BlockSpec(memory_space=pl.ANY)],
        out_specs=pl.BlockSpec((1, t8, MLA_HEADS * V_HEAD), lambda i, pt: (i, 0, 0)),
        scratch_shapes=[pltpu.VMEM((2, n_pages, page, KV_LORA), F32),
                        pltpu.VMEM((2, n_pages, page, QK_ROPE), F32),
                        pltpu.SemaphoreType.DMA((2, 2)),
                        pltpu.VMEM((rows, past), F32),
                        pltpu.VMEM((past, KV_LORA), BF16)],
    )
    return pl.pallas_call(
        functools.partial(_sattn_kernel, n_pages, scale),
        grid_spec=grid_spec,
        out_shape=jax.ShapeDtypeStruct((b, t8, MLA_HEADS * V_HEAD), BF16),
        compiler_params=pltpu.CompilerParams(dimension_semantics=("arbitrary",),
                                             vmem_limit_bytes=56 * 1024 * 1024),
        name="attn_sample",
    )(page_table, q_s, ckv8, kr8, w["w_uv_flat"], cache_lat, cache_kr)


def _memkv_kernel(m_ref, g_ref, w_ref, k_ref, v_ref):
    mn = _rms(m_ref[...], g_ref[...]).astype(BF16)
    half = k_ref.shape[1]
    k_ref[...] = _dot(mn, w_ref[:, 0:half])
    v_ref[...] = _dot(mn, w_ref[:, half:2 * half])


def _memory_kv(mem, w):
    n, d = mem.shape
    tm = min(512, n)
    xw = X_HEADS * X_HEAD_DIM
    row = lambda width: pl.BlockSpec((tm, width), lambda i: (i, 0))
    return pl.pallas_call(
        _memkv_kernel,
        grid=(n // tm,),
        in_specs=[row(d), _resident((1, d)), _resident(w["w_mem_kv"].shape)],
        out_specs=[row(xw), row(xw)],
        out_shape=[jax.ShapeDtypeStruct((n, xw), F32)] * 2,
        compiler_params=pltpu.CompilerParams(dimension_semantics=("arbitrary",)),
        name="memory_kv",
    )(mem, w["mem_norm"], w["w_mem_kv"])


def _memattn_kernel(scale, q_ref, k_ref, v_ref, o_ref):
    for i in range(q_ref.shape[0]):
        for h in range(X_HEADS):
            sl = slice(h * X_HEAD_DIM, (h + 1) * X_HEAD_DIM)
            s = _dot_nt(q_ref[i, :, sl].astype(BF16), k_ref[i, :, sl].astype(BF16)) * scale
            e = jnp.exp(s - jnp.max(s, axis=-1, keepdims=True))
            p = e / jnp.sum(e, axis=-1, keepdims=True)
            o_ref[i, :, sl] = _dot(p.astype(BF16), v_ref[i, :, sl].astype(BF16)).astype(o_ref.dtype)


def _memory_attention(qx, mem_k, mem_v):
    b, t, width = qx.shape
    n_mem = mem_k.shape[1]
    tq = min(512, t)
    bb = 1 if t >= 512 else 8
    scale = float(X_HEAD_DIM ** -0.5)
    return pl.pallas_call(
        functools.partial(_memattn_kernel, scale),
        grid=(b // bb, t // tq),
        in_specs=[pl.BlockSpec((bb, tq, width), lambda i, j: (i, j, 0)),
                  pl.BlockSpec((bb, n_mem, width), lambda i, j: (i, 0, 0)),
                  pl.BlockSpec((bb, n_mem, width), lambda i, j: (i, 0, 0))],
        out_specs=pl.BlockSpec((bb, tq, width), lambda i, j: (i, j, 0)),
        out_shape=jax.ShapeDtypeStruct((b, t, width), BF16),
        compiler_params=pltpu.CompilerParams(dimension_semantics=("arbitrary", "arbitrary")),
        name="memory_attn",
    )(qx, mem_k, mem_v)


def _merge_kernel(x_ref, py_ref, mo_ref, xo_ref, g_ref, wpo_ref, wmo_ref, wxo_ref, wo_ref, fn_ref,
                  h_ref, hnt_ref):
    d = x_ref.shape[1]
    merged = (g_ref[:, 0:d] * _dot(py_ref[...].astype(BF16), wpo_ref[...])
              + g_ref[:, d:2 * d] * _dot(mo_ref[...], wmo_ref[...])
              + g_ref[:, 2 * d:3 * d] * _dot(xo_ref[...], wxo_ref[...]))
    h = x_ref[...] + _dot(merged.astype(BF16), wo_ref[...])
    h_ref[...] = h
    hnt_ref[...] = _rms(h, fn_ref[...]).T.astype(BF16)


def _merge(x, pool_y, mla_o, mem_o, g, w, wmo):
    n, d = x.shape
    tm = min(512, n)
    row = lambda width: pl.BlockSpec((tm, width), lambda i: (i, 0))
    return pl.pallas_call(
        _merge_kernel,
        grid=(n // tm,),
        in_specs=[row(d), row(pool_y.shape[1]), row(mla_o.shape[1]), row(mem_o.shape[1]), row(3 * d),
                  _resident(w["w_pool_out"].shape), _resident(wmo.shape), _resident(w["w_x_out"].shape),
                  _resident(w["w_o"].shape), _resident((1, d))],
        out_specs=[row(d), pl.BlockSpec((d, tm), lambda i: (0, i))],
        out_shape=[jax.ShapeDtypeStruct((n, d), F32), jax.ShapeDtypeStruct((d, n), BF16)],
        compiler_params=pltpu.CompilerParams(dimension_semantics=("arbitrary",)),
        name="merge",
    )(x, pool_y, mla_o, mem_o, g, w["w_pool_out"], wmo, w["w_x_out"], w["w_o"], w["ffn_norm"])


_ROUTE_T = 512
_NOT_TOP = 99.0


def _top_k_columns(s):
    nk = s.shape[0]
    kidx = lax.broadcasted_iota(jnp.int32, s.shape, 0).astype(F32)
    rank = jnp.full(s.shape, _NOT_TOP, F32)
    vals = []
    for r in range(PEER_TOPK):
        m = jnp.max(s, axis=0, keepdims=True)
        first = jnp.min(jnp.where(s == m, kidx, float(nk)), axis=0, keepdims=True)
        sel = kidx == first
        rank = jnp.where(sel, float(r), rank)
        s = jnp.where(sel, NEG_INF, s)
        vals.append(m)
    return jnp.concatenate(vals, axis=0), rank


def _route_kernel(hnt_ref, wqt_ref, k1_ref, k2_ref, nb1_ref, e1_ref, r2_ref, e2_ref,
                  s_scr, rank_scr, val_scr, exp_scr):
    ncol = s_scr.shape[1]
    qt = _dot(wqt_ref[...], hnt_ref[...]).astype(BF16)
    for h in range(PEER_HEADS):
        q0 = h * 2 * PEER_HALF
        s1 = _dot(k1_ref[...], qt[q0:q0 + PEER_HALF, :])
        s2 = _dot(k2_ref[...], qt[q0 + PEER_HALF:q0 + 2 * PEER_HALF, :])
        for c in range(ncol):
            s_scr[2 * h, c] = s1[:, c * LANES:(c + 1) * LANES]
            s_scr[2 * h + 1, c] = s2[:, c * LANES:(c + 1) * LANES]

    def topk_body(i, carry):
        hh = i // ncol
        c = i % ncol
        s = s_scr[hh, c]
        vals, rank = _top_k_columns(s)
        val_scr[hh, c] = vals
        rank_scr[hh, c] = rank
        exp_scr[hh, c] = jnp.exp(s - vals[0:1, :])
        return carry

    lax.fori_loop(0, 2 * PEER_HEADS * ncol, topk_body, 0)

    def merge_body(i, carry):
        h = i // ncol
        c = i % ncol
        v1 = val_scr[2 * h, c]
        v2 = val_scr[2 * h + 1, c]
        aidx = lax.broadcasted_iota(jnp.int32, v1.shape, 0).astype(F32)
        kf = float(PEER_TOPK)
        taken = jnp.zeros(v1.shape, F32)
        front = v1 + v2[0:1, :]
        top = front[0:1, :]
        z = jnp.zeros_like(top)
        for r in range(PEER_TOPK):
            m = jnp.max(front, axis=0, keepdims=True)
            a_star = jnp.min(jnp.where(front == m, aidx, kf), axis=0, keepdims=True)
            sel = aidx == a_star
            z = z + jnp.exp(m - top)
            taken = taken + jnp.where(sel, 1.0, 0.0)
            nxt = jnp.sum(jnp.where(sel, taken, 0.0), axis=0, keepdims=True)
            v2n = jnp.sum(jnp.where(aidx == nxt, v2, 0.0), axis=0, keepdims=True)
            v2n = jnp.where(nxt >= kf, NEG_INF, v2n)
            front = jnp.where(sel, v1 + v2n, front)
        inv_z = 1.0 / z
        rank1 = rank_scr[2 * h, c]
        nb1 = jnp.zeros(rank1.shape, F32)
        for a in range(PEER_TOPK):
            nb1 = nb1 + jnp.where(rank1 == float(a), taken[a:a + 1, :], 0.0)
        nb1_ref[h, c] = nb1
        e1_ref[h, c] = exp_scr[2 * h, c] * inv_z
        r2_ref[h, c] = rank_scr[2 * h + 1, c]
        e2_ref[h, c] = exp_scr[2 * h + 1, c]
        return carry

    lax.fori_loop(0, PEER_HEADS * ncol, merge_body, 0)


def _peer_route(hnt, w):
    d, n = hnt.shape
    t = _ROUTE_T
    ncol = t // LANES
    blk = pl.BlockSpec((PEER_HEADS, ncol, N_KEYS, LANES), lambda i: (0, i, 0, 0))
    shp = jax.ShapeDtypeStruct((PEER_HEADS, n // LANES, N_KEYS, LANES), F32)
    return pl.pallas_call(
        _route_kernel,
        grid=(n // t,),
        in_specs=[pl.BlockSpec((d, t), lambda i: (0, i)), _resident(w["w_peer_qt"].shape),
                  _resident((N_KEYS, PEER_HALF)), _resident((N_KEYS, PEER_HALF))],
        out_specs=[blk] * 4,
        out_shape=[shp] * 4,
        scratch_shapes=[pltpu.VMEM((2 * PEER_HEADS, ncol, N_KEYS, LANES), F32),
                        pltpu.VMEM((2 * PEER_HEADS, ncol, N_KEYS, LANES), F32),
                        pltpu.VMEM((2 * PEER_HEADS, ncol, PEER_TOPK, LANES), F32),
                        pltpu.VMEM((2 * PEER_HEADS, ncol, N_KEYS, LANES), F32)],
        compiler_params=pltpu.CompilerParams(dimension_semantics=("arbitrary",)),
        name="peer_route",
    )(hnt, w["w_peer_qt"], w["peer_k1"], w["peer_k2"])


_DENSE_T = 512
_DENSE_E = 1024
_DENSE_MM_T = 256


def _gelu(x):
    return 0.5 * x * (1.0 + lax.erf(x * float(1.0 / np.sqrt(2.0))))


def _dense_kernel(hnt_ref, u_ref, vt_ref, nb1_ref, e1_ref, r2_ref, e2_ref, h_ref, fn_ref, y_ref, acc_ref):
    e = pl.program_id(1)
    t = hnt_ref.shape[1]
    n_i1 = u_ref.shape[0] // N_KEYS
    cols_per_mm = _DENSE_MM_T // LANES

    @pl.when(e == 0)
    def _():
        acc_ref[...] = jnp.zeros(acc_ref.shape, F32)

    for cc in range(t // _DENSE_MM_T):
        tsl = slice(cc * _DENSE_MM_T, (cc + 1) * _DENSE_MM_T)
        st = _dot(u_ref[...], hnt_ref[:, tsl])
        p_rows = []
        for j in range(n_i1):
            a = _gelu(st[j * N_KEYS:(j + 1) * N_KEYS, :])
            parts = []
            for c2 in range(cols_per_mm):
                c = cc * cols_per_mm + c2
                g = jnp.zeros((N_KEYS, LANES), F32)
                for h in range(PEER_HEADS):
                    nb = nb1_ref[h, c, j:j + 1, :]
                    e1 = e1_ref[h, c, j:j + 1, :]
                    g = g + jnp.where(r2_ref[h, c] < nb, e2_ref[h, c] * e1, 0.0)
                parts.append(g)
            p_rows.append((jnp.concatenate(parts, axis=1) * a).astype(BF16))
        pt = jnp.concatenate(p_rows, axis=0)
        acc_ref[:, tsl] += _dot(vt_ref[...], pt)

    @pl.when(e == pl.num_programs(1) - 1)
    def _():
        y_ref[...] = _rms(h_ref[...] + acc_ref[...].T, fn_ref[...])


def _peer_dense(hnt, h, route, w):
    d, n = hnt.shape
    t = _DENSE_T
    ne = w["peer_u"].shape[0]
    ncol = t // LANES
    n_i1 = _DENSE_E // N_KEYS
    nb1, e1, r2, e2 = route
    row_blk = pl.BlockSpec((PEER_HEADS, ncol, n_i1, LANES), lambda i, e: (0, i, e, 0))
    full_blk = pl.BlockSpec((PEER_HEADS, ncol, N_KEYS, LANES), lambda i, e: (0, i, 0, 0))
    return pl.pallas_call(
        _dense_kernel,
        grid=(n // t, ne // _DENSE_E),
        in_specs=[pl.BlockSpec((d, t), lambda i, e: (0, i)),
                  pl.BlockSpec((_DENSE_E, d), lambda i, e: (e, 0)),
                  pl.BlockSpec((d, _DENSE_E), lambda i, e: (0, e)),
                  row_blk, row_blk, full_blk, full_blk,
                  pl.BlockSpec((t, d), lambda i, e: (i, 0)),
                  pl.BlockSpec((1, d), lambda i, e: (0, 0))],
        out_specs=pl.BlockSpec((t, d), lambda i, e: (i, 0)),
        out_shape=jax.ShapeDtypeStruct((n, d), F32),
        scratch_shapes=[pltpu.VMEM((d, t), F32)],
        compiler_params=pltpu.CompilerParams(dimension_semantics=("arbitrary", "arbitrary")),
        name="peer_dense",
    )(hnt, w["peer_u"], w["peer_vt"], nb1, e1, r2, e2, h, w["final_norm"])


def _rotate_half_cols(w):
    half = QK_ROPE // 2
    return jnp.concatenate([-w[..., half:], w[..., :half]], axis=-1)


def _prep_weights(attn_norm, w_in, q_norm, kv_norm, w_q_up, w_uk, w_uv, w_pool_grp, pool_scale, mem_norm,
                  w_mem_k, w_mem_v, w_pool_out, w_mla_out, w_x_out, w_o, ffn_norm, w_peer_q, peer_k1,
                  peer_k2, peer_u, peer_v, final_norm):
    d = w_in.shape[0]
    pool_w = POOL_GROUP * len(POOL_WINDOWS)
    xw = X_HEADS * X_HEAD_DIM
    o = np.cumsum([0, pool_w, Q_LORA, KV_LORA, QK_ROPE, xw, 3 * d])
    seg = [w_in[:, o[i]:o[i + 1]] for i in range(6)]
    kr = seg[3]

    def rope_slot(cols):
        z = jnp.zeros(cols.shape[:-1] + (QK_NOPE,), F32)
        z2 = jnp.zeros(cols.shape[:-1] + (HEAD_PAD - QK_NOPE - QK_ROPE,), F32)
        return jnp.concatenate([z, cols, z2], axis=-1)

    w_in_p = jnp.concatenate([seg[0], seg[1], seg[2], rope_slot(kr), rope_slot(_rotate_half_cols(kr)),
                              seg[4], seg[5]], axis=1).astype(BF16)
    q_nope = w_q_up[:, :, :QK_NOPE]
    q_rope = w_q_up[:, :, QK_NOPE:]
    tail = jnp.zeros(q_rope.shape[:2] + (HEAD_PAD - QK_NOPE - QK_ROPE,), F32)
    qa = jnp.concatenate([q_nope, q_rope, tail], axis=-1).reshape(Q_LORA, MLA_HEADS * HEAD_PAD)
    qb = rope_slot(_rotate_half_cols(q_rope)).reshape(Q_LORA, MLA_HEADS * HEAD_PAD)
    w_q = jnp.concatenate([qa, qb], axis=1).astype(BF16)
    padk = jnp.zeros((KV_LORA, MLA_HEADS, HEAD_PAD - QK_NOPE), F32)
    padv = jnp.zeros((KV_LORA, MLA_HEADS, HEAD_PAD - V_HEAD), F32)
    w_kv = jnp.concatenate([jnp.concatenate([w_uk, padk], -1).reshape(KV_LORA, -1),
                            jnp.concatenate([w_uv, padv], -1).reshape(KV_LORA, -1)], axis=1).astype(BF16)
    ukt = jnp.transpose(w_uk, (1, 2, 0))
    top = jnp.concatenate([ukt, jnp.zeros((MLA_HEADS, QK_NOPE, LANES), F32)], axis=-1)
    mid = jnp.concatenate([jnp.zeros((QK_ROPE, KV_LORA), F32), jnp.eye(QK_ROPE, dtype=F32),
                           jnp.zeros((QK_ROPE, LANES - QK_ROPE), F32)], axis=-1)
    mid = jnp.broadcast_to(mid[None], (MLA_HEADS,) + mid.shape)
    bot = jnp.zeros((MLA_HEADS, HEAD_PAD - QK_NOPE - QK_ROPE, KV_LORA + LANES), F32)
    w_abs = jnp.concatenate([top, mid, bot], axis=1).astype(BF16)
    w_mla_pad = jnp.concatenate([w_mla_out.reshape(MLA_HEADS, V_HEAD, d),
                                 jnp.zeros((MLA_HEADS, HEAD_PAD - V_HEAD, d), F32)], axis=1)
    return dict(
        attn_norm=attn_norm.reshape(1, d), w_in=w_in_p, q_norm=q_norm.reshape(1, -1),
        kv_norm=kv_norm.reshape(1, -1), w_q=w_q, w_kv=w_kv, w_abs=w_abs,
        w_uv_flat=w_uv.reshape(KV_LORA, MLA_HEADS * V_HEAD).astype(BF16),
        w_pool_grp=w_pool_grp.astype(BF16), pool_scale=pool_scale.reshape(1, -1),
        mem_norm=mem_norm.reshape(1, d),
        w_mem_kv=jnp.concatenate([w_mem_k.reshape(d, xw), w_mem_v.reshape(d, xw)], axis=1).astype(BF16),
        w_pool_out=w_pool_out.astype(BF16), w_mla_out=w_mla_out.astype(BF16),
        w_mla_out_pad=w_mla_pad.reshape(MLA_HEADS * HEAD_PAD, d).astype(BF16),
        w_x_out=w_x_out.astype(BF16), w_o=w_o.astype(BF16), ffn_norm=ffn_norm.reshape(1, d),
        w_peer_qt=w_peer_q.reshape(d, -1).T.astype(BF16), peer_k1=peer_k1.astype(BF16),
        peer_k2=peer_k2.astype(BF16), peer_u=peer_u.astype(BF16), peer_vt=peer_v.T.astype(BF16),
        final_norm=final_norm.reshape(1, d),
    )


def _rope_tables(start, t):
    half = QK_ROPE // 2
    inv = jnp.power(ROPE_BASE, -jnp.arange(half, dtype=F32) / half)
    pos = (start + jnp.arange(t, dtype=jnp.int32)).astype(F32)
    ang = pos[:, None] * inv[None, :]
    cos, sin = jnp.cos(ang), jnp.sin(ang)
    one = jnp.ones((t, QK_NOPE), F32)
    z_nope = jnp.zeros((t, QK_NOPE), F32)
    z_tail = jnp.zeros((t, HEAD_PAD - QK_NOPE - QK_ROPE), F32)
    return (jnp.concatenate([one, cos, cos, z_tail], axis=1),
            jnp.concatenate([z_nope, sin, sin, z_tail], axis=1))


def _peer_and_norm(h, hnt, w):
    return _peer_dense(hnt, h, _peer_route(hnt, w), w)


def kernel(x_prompt, x_sample, cache_kv_latent, cache_k_rope, state_pool, cache_mem_k, cache_mem_v, page_table, mem_prompt, attn_norm, w_in, q_norm, kv_norm, w_q_up, w_uk, w_uv, w_pool_grp, pool_scale, mem_norm, w_mem_k, w_mem_v, w_pool_out, w_mla_out, w_x_out, w_o, ffn_norm, w_peer_q, peer_k1, peer_k2, peer_u, peer_v, final_norm):
    depth = attn_norm.shape[0]
    assert depth == 1, "single trunk layer"
    b, s, d = x_prompt.shape
    bd, td, _ = x_sample.shape
    n_pages = page_table.shape[1]
    past = n_pages * cache_kv_latent.shape[2]
    xw = X_HEADS * X_HEAD_DIM
    w = _prep_weights(attn_norm[0], w_in[0], q_norm[0], kv_norm[0], w_q_up[0], w_uk[0], w_uv[0],
                      w_pool_grp[0], pool_scale[0], mem_norm[0], w_mem_k[0], w_mem_v[0], w_pool_out[0],
                      w_mla_out[0], w_x_out[0], w_o[0], ffn_norm[0], w_peer_q[0], peer_k1[0], peer_k2[0],
                      peer_u[0], peer_v[0], final_norm)
    rope_lanes = slice(QK_NOPE, QK_NOPE + QK_ROPE)

    xp = x_prompt.reshape(b * s, d)
    cos_p, sin_p = _rope_tables(0, s)
    tm = min(512, b * s)
    u, c_kv, kr, qx, g, q_att, k_att, v_att = _proj_in(xp, cos_p, sin_p, s // tm, w, sample=False)
    u3 = u.reshape(b, s, -1)
    pool_y = _pool_mix(u3, jnp.zeros((b, POOL_HIST, u3.shape[-1]), F32), 0, w)
    mla_o = _prompt_attention(q_att.reshape(b, s, -1), k_att.reshape(b, s, -1), v_att.reshape(b, s, -1))
    n_mem = mem_prompt.shape[1]
    mem_k, mem_v = _memory_kv(mem_prompt.reshape(b * n_mem, d), w)
    mem_o = _memory_attention(qx.reshape(b, s, xw), mem_k.reshape(b, n_mem, xw), mem_v.reshape(b, n_mem, xw))
    h, hnt = _merge(xp, pool_y.reshape(b * s, -1), mla_o.reshape(b * s, -1), mem_o.reshape(b * s, xw), g, w,
                    w["w_mla_out_pad"])
    y_prompt = _peer_and_norm(h, hnt, w).reshape(b, s, d)

    xs = x_sample.reshape(bd * td, d)
    cos_s, sin_s = _rope_tables(past, td)
    reps = (bd * td) // td
    cos_s = jnp.tile(cos_s, (reps, 1))
    sin_s = jnp.tile(sin_s, (reps, 1))
    us, c_kv_s, kr_s, qx_s, g_s, q_s = _proj_in(xs, cos_s, sin_s, 1, w, sample=True)
    us3 = us.reshape(bd, td, -1)
    pool_y_s = _pool_mix(us3, state_pool[0], past, w)
    t8 = 8
    q_rows = q_s.reshape(bd, td, MLA_HEADS, 384).transpose(0, 2, 1, 3)
    q_rows = jnp.pad(q_rows, ((0, 0), (0, 0), (0, t8 - td), (0, 0))).reshape(bd, MLA_HEADS * t8, 384)
    ckv8 = jnp.pad(c_kv_s.reshape(bd, td, KV_LORA), ((0, 0), (0, t8 - td), (0, 0)))
    kr_new = kr_s[:, rope_lanes].reshape(bd, td, QK_ROPE)
    kr8 = jnp.pad(kr_new, ((0, 0), (0, t8 - td), (0, 0)))
    mla_o_s = _sample_attention(q_rows, ckv8, kr8, page_table, cache_kv_latent[0], cache_k_rope[0], w)
    mla_o_s = mla_o_s[:, :td].reshape(bd * td, -1)
    mem_o_s = _memory_attention(qx_s.reshape(bd, td, xw), cache_mem_k[0].reshape(bd, -1, xw),
                                cache_mem_v[0].reshape(bd, -1, xw))
    h_s, hnt_s = _merge(xs, pool_y_s.reshape(bd * td, -1), mla_o_s, mem_o_s.reshape(bd * td, xw), g_s, w,
                        w["w_mla_out"])
    y_sample = _peer_and_norm(h_s, hnt_s, w).reshape(bd, td, d)

    new_pool_s = jnp.concatenate([state_pool[0], us3], axis=1)[:, -POOL_HIST:]
    return (y_prompt, y_sample,
            c_kv.reshape(1, b, s, KV_LORA), kr[:, rope_lanes].reshape(1, b, s, QK_ROPE),
            u3[:, -POOL_HIST:][None],
            mem_k.reshape(1, b, n_mem, X_HEADS, X_HEAD_DIM), mem_v.reshape(1, b, n_mem, X_HEADS, X_HEAD_DIM),
            c_kv_s.reshape(1, bd, td, KV_LORA), kr_new[None], new_pool_s[None])
```

```python
import functools

import jax
import jax.numpy as jnp
import numpy as np
from jax import lax
from jax.experimental import pallas as pl
from jax.experimental.pallas import tpu as pltpu

F32 = jnp.float32
BF16 = jnp.bfloat16

EPS = 1e-6
POOL_WINDOWS = (2, 4, 8, 16)
POOL_GROUP = 128
POOL_HIST = max(POOL_WINDOWS) - 1
MLA_HEADS = 8
Q_LORA = 384
KV_LORA = 256
QK_NOPE = 64
QK_ROPE = 32
V_HEAD = 64
ROPE_BASE = 10000.0
X_HEADS = 4
X_HEAD_DIM = 128
PEER_HEADS = 8
N_KEYS = 128
PEER_HALF = 128
PEER_TOPK = 16

LANES = 128
HEAD_PAD = 128

NT_DIMS = (((1,), (1,)), ((), ()))

NEG_INF = float("-inf")


def _rms(x, g):
    return x * lax.rsqrt(jnp.mean(x * x, axis=-1, keepdims=True) + EPS) * g


def _dot(a, b):
    return jnp.dot(a, b, preferred_element_type=F32)


def _dot_nt(a, b):
    return lax.dot_general(a, b, NT_DIMS, preferred_element_type=F32)


def _pack_bf16(x):
    return pltpu.bitcast(x.astype(BF16), jnp.uint32)


def _unpack_bf16(x):
    return pltpu.bitcast(x, BF16)


def _resident(shape):
    nd = len(shape)
    return pl.BlockSpec(shape, lambda *_: (0,) * nd, pipeline_mode=pl.Buffered(1))


_C_U = (0, 512)
_C_CQ = (512, 896)
_C_CKV = (896, 1152)
_C_KRA = (1152, 1280)
_C_KRB = (1280, 1408)
_C_QX = (1408, 1920)
_C_G = (1920, 4992)


def _proj_kernel(sample, x_ref, cos_ref, sin_ref, an_ref, win_ref, qn_ref, kvn_ref, wq_ref, wx_ref,
                 u_ref, ckv_ref, kr_ref, qx_ref, g_ref, *outs):
    xn = _rms(x_ref[...], an_ref[...]).astype(BF16)

    def seg(c):
        return _dot(xn, win_ref[:, c[0]:c[1]])

    u_ref[...] = seg(_C_U)
    qx_ref[...] = seg(_C_QX)
    g_ref[...] = jax.nn.sigmoid(seg(_C_G))
    cos = cos_ref[...]
    sin = sin_ref[...]
    kr = seg(_C_KRA) * cos + seg(_C_KRB) * sin
    kr_ref[...] = kr
    c_kv = _rms(seg(_C_CKV), kvn_ref[...])
    ckv_ref[...] = c_kv
    cqn = _rms(seg(_C_CQ), qn_ref[...]).astype(BF16)
    nq = MLA_HEADS * HEAD_PAD
    qa = _dot(cqn, wq_ref[:, 0:nq])
    qb = _dot(cqn, wq_ref[:, nq:2 * nq])
    if sample:
        (qs_ref,) = outs
        for h in range(MLA_HEADS):
            sl = slice(h * HEAD_PAD, (h + 1) * HEAD_PAD)
            qh = (qa[:, sl] * cos + qb[:, sl] * sin).astype(BF16)
            qs_ref[:, h * 384:(h + 1) * 384] = _dot(qh, wx_ref[h])
    else:
        q_ref, k_ref, v_ref = outs
        kv = _dot(c_kv.astype(BF16), wx_ref[...])
        for h in range(MLA_HEADS):
            sl = slice(h * HEAD_PAD, (h + 1) * HEAD_PAD)
            q_ref[:, sl] = (qa[:, sl] * cos + qb[:, sl] * sin).astype(BF16)
            k_ref[:, sl] = (kv[:, sl] + kr).astype(BF16)
        v_ref[...] = kv[:, nq:2 * nq].astype(BF16)


def _proj_in(x, cos, sin, n_pos_tiles, w, sample):
    n, d = x.shape
    tm = min(512, n)
    nq = MLA_HEADS * HEAD_PAD
    row = lambda width: pl.BlockSpec((tm, width), lambda i: (i, 0))
    pos = pl.BlockSpec((tm, LANES), lambda i: (i % n_pos_tiles, 0))
    wx = w["w_abs"] if sample else w["w_kv"]
    in_specs = [row(d), pos, pos, _resident((1, d)), _resident(w["w_in"].shape), _resident((1, Q_LORA)),
                _resident((1, KV_LORA)), _resident(w["w_q"].shape), _resident(wx.shape)]
    out_shape = [jax.ShapeDtypeStruct((n, 512), F32), jax.ShapeDtypeStruct((n, KV_LORA), F32),
                 jax.ShapeDtypeStruct((n, LANES), F32), jax.ShapeDtypeStruct((n, 512), F32),
                 jax.ShapeDtypeStruct((n, 3 * d), F32)]
    out_specs = [row(512), row(KV_LORA), row(LANES), row(512), row(3 * d)]
    if sample:
        out_shape.append(jax.ShapeDtypeStruct((n, MLA_HEADS * 384), F32))
        out_specs.append(row(MLA_HEADS * 384))
    else:
        out_shape += [jax.ShapeDtypeStruct((n, nq), BF16)] * 3
        out_specs += [row(nq)] * 3
    return pl.pallas_call(
        functools.partial(_proj_kernel, sample),
        grid=(n // tm,),
        in_specs=in_specs,
        out_specs=out_specs,
        out_shape=out_shape,
        compiler_params=pltpu.CompilerParams(dimension_semantics=("arbitrary",)),
        name="proj_in_s" if sample else "proj_in_p",
    )(x, cos, sin, w["attn_norm"], w["w_in"], w["q_norm"], w["kv_norm"], w["w_q"], wx)


def _pool_kernel(start, u_ref, hist_ref, wg_ref, sc_ref, o_ref, ext_ref):
    j = pl.program_id(1)
    tm = u_ref.shape[1]
    base = POOL_HIST + 1

    @pl.when(j == 0)
    def _():
        ext_ref[0:1, :] = jnp.zeros((1, ext_ref.shape[1]), F32)
        ext_ref[1:base, :] = hist_ref[0]

    @pl.when(j > 0)
    def _():
        ext_ref[0:base, :] = ext_ref[tm:tm + base, :]

    ext_ref[base:base + tm, :] = u_ref[0]
    pos = start + j * tm + lax.broadcasted_iota(jnp.int32, (tm, 1), 0)
    for g, w in enumerate(POOL_WINDOWS):
        sl = slice(g * POOL_GROUP, (g + 1) * POOL_GROUP)
        cur = ext_ref[base:base + tm, sl]
        acc = cur
        for jj in range(1, w):
            acc = acc + ext_ref[base - jj:base - jj + tm, sl]
        cnt = jnp.minimum(pos + 1, w).astype(F32)
        dlt = acc / cnt - cur
        o_ref[0, :, sl] = _dot(dlt.astype(BF16), wg_ref[g]) * sc_ref[:, sl]


def _pool_mix(u, hist, start, w):
    b, t, width = u.shape
    tm = min(512, t)
    return pl.pallas_call(
        functools.partial(_pool_kernel, start),
        grid=(b, t // tm),
        in_specs=[pl.BlockSpec((1, tm, width), lambda i, j: (i, j, 0)),
                  pl.BlockSpec((1, POOL_HIST, width), lambda i, j: (i, 0, 0)),
                  _resident(w["w_pool_grp"].shape), _resident((1, width))],
        out_specs=pl.BlockSpec((1, tm, width), lambda i, j: (i, j, 0)),
        out_shape=jax.ShapeDtypeStruct((b, t, width), F32),
        scratch_shapes=[pltpu.VMEM((POOL_HIST + 1 + tm, width), F32)],
        compiler_params=pltpu.CompilerParams(dimension_semantics=("arbitrary", "arbitrary")),
        name="pool_mix",
    )(u, hist, w["w_pool_grp"], w["pool_scale"])


def _lane_fold(x, op):
    out = x[:, 0:LANES]
    for i in range(1, x.shape[1] // LANES):
        out = op(out, x[:, i * LANES:(i + 1) * LANES])
    return out


def _attn_kernel(exp2_scale, q_ref, k_ref, v_ref, o_ref, s_scr, m_scr, l_scr, acc_scr):
    qi = pl.program_id(1)
    tq = q_ref.shape[1]
    tk = tq
    below_diag = (lax.broadcasted_iota(jnp.int32, (tq, tk), 1)
                  <= lax.broadcasted_iota(jnp.int32, (tq, tk), 0))
    heads = [slice(h * HEAD_PAD, (h + 1) * HEAD_PAD) for h in range(MLA_HEADS)]
    m_scr[...] = jnp.full(m_scr.shape, NEG_INF, F32)
    l_scr[...] = jnp.zeros(l_scr.shape, F32)
    acc_scr[...] = jnp.zeros(acc_scr.shape, F32)

    def scores(j, carry):
        k0 = pl.multiple_of(j * tk, tk)
        for h, sl in enumerate(heads):
            s = _dot_nt(q_ref[0, :, sl], k_ref[0, pl.ds(k0, tk), sl])
            s_scr[j, h] = s
            m_scr[h] = jnp.maximum(m_scr[h], _lane_fold(s, jnp.maximum))
        return carry

    lax.fori_loop(0, qi, scores, 0)
    k0 = pl.multiple_of(qi * tk, tk)
    for h, sl in enumerate(heads):
        s = jnp.where(below_diag, _dot_nt(q_ref[0, :, sl], k_ref[0, pl.ds(k0, tk), sl]), NEG_INF)
        s_scr[qi, h] = s
        m_row = jnp.max(jnp.maximum(m_scr[h], _lane_fold(s, jnp.maximum)), axis=-1, keepdims=True)
        m_scr[h] = jnp.broadcast_to(m_row, (tq, HEAD_PAD))

    def weighted(j, carry):
        k0 = pl.multiple_of(j * tk, tk)
        for h, sl in enumerate(heads):
            p = jnp.exp2((s_scr[j, h] - m_scr[h][:, 0:1]) * exp2_scale)
            l_scr[h] += _lane_fold(p, jnp.add)
            acc_scr[h] += _dot(p.astype(BF16), v_ref[0, pl.ds(k0, tk), sl])
        return carry

    lax.fori_loop(0, qi + 1, weighted, 0)
    for h, sl in enumerate(heads):
        l_row = jnp.sum(l_scr[h], axis=-1, keepdims=True)
        o_ref[0, :, sl] = (acc_scr[h] / l_row).astype(o_ref.dtype)


def _prompt_attention(q, k, v):
    b, s, width = q.shape
    tq = 256
    exp2_scale = float((QK_NOPE + QK_ROPE) ** -0.5 * np.log2(np.e))
    return pl.pallas_call(
        functools.partial(_attn_kernel, exp2_scale),
        grid=(b, s // tq),
        in_specs=[pl.BlockSpec((1, tq, width), lambda i, j: (i, j, 0)),
                  pl.BlockSpec((1, s, width), lambda i, j: (i, 0, 0)),
                  pl.BlockSpec((1, s, width), lambda i, j: (i, 0, 0))],
        out_specs=pl.BlockSpec((1, tq, width), lambda i, j: (i, j, 0)),
        out_shape=jax.ShapeDtypeStruct((b, s, width), BF16),
        scratch_shapes=([pltpu.VMEM((s // tq, MLA_HEADS, tq, tq), F32)]
                        + [pltpu.VMEM((MLA_HEADS, tq, HEAD_PAD), F32)] * 3),
        compiler_params=pltpu.CompilerParams(dimension_semantics=("arbitrary", "arbitrary"),
                                             vmem_limit_bytes=56 * 1024 * 1024),
        name="attn_prompt",
    )(q, k, v)


_PAGES_PER_CHUNK = 8


def _sattn_kernel(n_pages, exp2_scale, pt_ref, q_ref, ckv_ref, krn_ref, wuv_ref, lat_hbm, kr_hbm,
                  o_ref, latbuf, krbuf, sem, s_scr, lat16):
    b = pl.program_id(0)
    nb = pl.num_programs(0)
    slot = b % 2
    page = latbuf.shape[2]

    def page_copies(bb, sl):
        cps = []
        for p in range(n_pages):
            pg = pt_ref[bb, p]
            cps.append(pltpu.make_async_copy(lat_hbm.at[pg], latbuf.at[sl, p], sem.at[0, sl]))
            cps.append(pltpu.make_async_copy(kr_hbm.at[pg], krbuf.at[sl, p], sem.at[1, sl]))
        return cps

    @pl.when(b == 0)
    def _():
        for cp in page_copies(0, 0):
            cp.start()

    @pl.when(b + 1 < nb)
    def _():
        for cp in page_copies(b + 1, 1 - slot):
            cp.start()

    for cp in page_copies(b, slot):
        cp.wait()

    q = q_ref[0]
    ql = q[:, 0:KV_LORA].astype(BF16)
    qr = q[:, KV_LORA:KV_LORA + QK_ROPE].astype(BF16)
    ck = _PAGES_PER_CHUNK * page
    for c in range(n_pages // _PAGES_PER_CHUNK):
        pages = range(c * _PAGES_PER_CHUNK, (c + 1) * _PAGES_PER_CHUNK)
        lb = latbuf[slot, pages.start:pages.stop].reshape(ck, KV_LORA).astype(BF16)
        kb = jnp.concatenate([krbuf[slot, p] for p in pages], axis=1).astype(BF16)
        lat16[c * ck:(c + 1) * ck, :] = lb
        s_scr[:, c * ck:(c + 1) * ck] = _dot_nt(ql, lb) + _dot(qr, kb)
    cn = ckv_ref[0].astype(BF16)
    kn = krn_ref[0].astype(BF16)
    rows = q.shape[0]
    t_new = cn.shape[0]
    sn = _dot_nt(ql, cn) + _dot_nt(qr, kn)
    assert t_new & (t_new - 1) == 0 and V_HEAD & (V_HEAD - 1) == 0
    t_of_row = lax.broadcasted_iota(jnp.int32, (rows, t_new), 0) & (t_new - 1)
    key = lax.broadcasted_iota(jnp.int32, (rows, t_new), 1)
    sn = jnp.where(key <= t_of_row, sn, NEG_INF)
    s = s_scr[...]
    m = jnp.maximum(jnp.max(s, axis=-1, keepdims=True), jnp.max(sn, axis=-1, keepdims=True))
    p = jnp.exp2((s - m) * exp2_scale)
    pn = jnp.exp2((sn - m) * exp2_scale)
    inv = 1.0 / (jnp.sum(p, axis=-1, keepdims=True) + jnp.sum(pn, axis=-1, keepdims=True))
    o = _dot((p * inv).astype(BF16), lat16[...])
    pnb = (pn * inv).astype(BF16).astype(F32)
    cnf = cn.astype(F32)
    for t in range(t_new):
        o = o + pnb[:, t:t + 1] * cnf[t:t + 1, :]
    full = _dot(o.astype(BF16), wuv_ref[...])
    lane_head = lax.shift_right_logical(lax.broadcasted_iota(jnp.int32, (t_new, full.shape[1]), 1),
                                        int(np.log2(V_HEAD)))
    out = jnp.zeros((t_new, full.shape[1]), F32)
    for h in range(MLA_HEADS):
        out = out + jnp.where(lane_head == h, full[h * t_new:(h + 1) * t_new, :], 0.0)
    o_ref[0] = out.astype(o_ref.dtype)


def _sample_attention(q_s, ckv8, kr8, page_table, cache_lat, cache_kr_t, w):
    b, rows, qw = q_s.shape
    n_pages = page_table.shape[1]
    page = cache_lat.shape[1]
    past = n_pages * page
    exp2_scale = float((QK_NOPE + QK_ROPE) ** -0.5 * np.log2(np.e))
    t8 = ckv8.shape[1]
    grid_spec = pltpu.PrefetchScalarGridSpec(
        num_scalar_prefetch=1,
        grid=(b,),
        in_specs=[pl.BlockSpec((1, rows, qw), lambda i, pt: (i, 0, 0)),
                  pl.BlockSpec((1, t8, KV_LORA), lambda i, pt: (i, 0, 0)),
                  pl.BlockSpec((1, t8, QK_ROPE), lambda i, pt: (i, 0, 0)),
                  pl.BlockSpec(w["w_uv_flat"].shape, lambda i, pt: (0, 0)),
                  pl.BlockSpec(memory_space=pl.ANY),
                  pl.BlockSpec(memory_space=pl.ANY)],
        out_specs=pl.BlockSpec((1, t8, MLA_HEADS * V_HEAD), lambda i, pt: (i, 0, 0)),
        scratch_shapes=[pltpu.VMEM((2, n_pages, page, KV_LORA), F32),
                        pltpu.VMEM((2, n_pages, QK_ROPE, page), F32),
                        pltpu.SemaphoreType.DMA((2, 2)),
                        pltpu.VMEM((rows, past), F32),
                        pltpu.VMEM((past, KV_LORA), BF16)],
    )
    return pl.pallas_call(
        functools.partial(_sattn_kernel, n_pages, exp2_scale),
        grid_spec=grid_spec,
        out_shape=jax.ShapeDtypeStruct((b, t8, MLA_HEADS * V_HEAD), BF16),
        compiler_params=pltpu.CompilerParams(dimension_semantics=("arbitrary",),
                                             vmem_limit_bytes=56 * 1024 * 1024),
        name="attn_sample",
    )(page_table, q_s, ckv8, kr8, w["w_uv_flat"], cache_lat, cache_kr_t)


def _memkv_kernel(m_ref, g_ref, w_ref, k_ref, v_ref):
    mn = _rms(m_ref[...], g_ref[...]).astype(BF16)
    half = k_ref.shape[1]
    k_ref[...] = _dot(mn, w_ref[:, 0:half])
    v_ref[...] = _dot(mn, w_ref[:, half:2 * half])


def _memory_kv(mem, w):
    n, d = mem.shape
    tm = min(512, n)
    xw = X_HEADS * X_HEAD_DIM
    row = lambda width: pl.BlockSpec((tm, width), lambda i: (i, 0))
    return pl.pallas_call(
        _memkv_kernel,
        grid=(n // tm,),
        in_specs=[row(d), _resident((1, d)), _resident(w["w_mem_kv"].shape)],
        out_specs=[row(xw), row(xw)],
        out_shape=[jax.ShapeDtypeStruct((n, xw), F32)] * 2,
        compiler_params=pltpu.CompilerParams(dimension_semantics=("arbitrary",)),
        name="memory_kv",
    )(mem, w["mem_norm"], w["w_mem_kv"])


def _memattn_kernel(scale, q_ref, k_ref, v_ref, o_ref):
    for i in range(q_ref.shape[0]):
        for h in range(X_HEADS):
            sl = slice(h * X_HEAD_DIM, (h + 1) * X_HEAD_DIM)
            s = _dot_nt(q_ref[i, :, sl].astype(BF16), k_ref[i, :, sl].astype(BF16)) * scale
            e = jnp.exp(s - jnp.max(s, axis=-1, keepdims=True))
            p = e / jnp.sum(e, axis=-1, keepdims=True)
            o_ref[i, :, sl] = _dot(p.astype(BF16), v_ref[i, :, sl].astype(BF16)).astype(o_ref.dtype)


def _memory_attention(qx, mem_k, mem_v):
    b, t, width = qx.shape
    n_mem = mem_k.shape[1]
    tq = min(512, t)
    scale = float(X_HEAD_DIM ** -0.5)
    return pl.pallas_call(
        functools.partial(_memattn_kernel, scale),
        grid=(b, t // tq),
        in_specs=[pl.BlockSpec((1, tq, width), lambda i, j: (i, j, 0)),
                  pl.BlockSpec((1, n_mem, width), lambda i, j: (i, 0, 0)),
                  pl.BlockSpec((1, n_mem, width), lambda i, j: (i, 0, 0))],
        out_specs=pl.BlockSpec((1, tq, width), lambda i, j: (i, j, 0)),
        out_shape=jax.ShapeDtypeStruct((b, t, width), BF16),
        compiler_params=pltpu.CompilerParams(dimension_semantics=("arbitrary", "arbitrary")),
        name="memory_attn",
    )(qx, mem_k, mem_v)


_MEM_S_BATCH = 8


def _memattn_rows_kernel(scale, q_ref, k_ref, v_ref, o_ref):
    rows, nk = q_ref.shape[1], k_ref.shape[1]
    assert X_HEADS & (X_HEADS - 1) == 0
    same_head = ((lax.broadcasted_iota(jnp.int32, (rows, nk), 0) & (X_HEADS - 1))
                 == (lax.broadcasted_iota(jnp.int32, (rows, nk), 1) & (X_HEADS - 1)))
    for i in range(q_ref.shape[0]):
        s = _dot_nt(q_ref[i].astype(BF16), k_ref[i].astype(BF16)) * scale
        s = jnp.where(same_head, s, NEG_INF)
        e = jnp.exp(s - jnp.max(s, axis=-1, keepdims=True))
        p = e / jnp.sum(e, axis=-1, keepdims=True)
        o_ref[i] = _dot(p.astype(BF16), v_ref[i].astype(BF16)).astype(o_ref.dtype)


def _memory_attention_rows(q_rows, mem_k, mem_v):
    b, rows, hd = q_rows.shape
    nk = mem_k.shape[1]
    bb = _MEM_S_BATCH
    scale = float(X_HEAD_DIM ** -0.5)
    blk = lambda r: pl.BlockSpec((bb, r, hd), lambda i: (i, 0, 0))
    return pl.pallas_call(
        functools.partial(_memattn_rows_kernel, scale),
        grid=(b // bb,),
        in_specs=[blk(rows), blk(nk), blk(nk)],
        out_specs=blk(rows),
        out_shape=jax.ShapeDtypeStruct((b, rows, hd), BF16),
        compiler_params=pltpu.CompilerParams(dimension_semantics=("arbitrary",)),
        name="memory_attn_rows",
    )(q_rows, mem_k, mem_v)


def _merge_kernel(x_ref, py_ref, mo_ref, xo_ref, g_ref, wpo_ref, wmo_ref, wxo_ref, wo_ref, fn_ref,
                  h_ref, hnt_ref):
    d = x_ref.shape[1]
    merged = (g_ref[:, 0:d] * _dot(py_ref[...].astype(BF16), wpo_ref[...])
              + g_ref[:, d:2 * d] * _dot(mo_ref[...], wmo_ref[...])
              + g_ref[:, 2 * d:3 * d] * _dot(xo_ref[...], wxo_ref[...]))
    h = x_ref[...] + _dot(merged.astype(BF16), wo_ref[...])
    h_ref[...] = h
    hnt_ref[...] = _rms(h, fn_ref[...]).T.astype(BF16)


def _merge(x, pool_y, mla_o, mem_o, g, w, wmo):
    n, d = x.shape
    tm = min(512, n)
    row = lambda width: pl.BlockSpec((tm, width), lambda i: (i, 0))
    return pl.pallas_call(
        _merge_kernel,
        grid=(n // tm,),
        in_specs=[row(d), row(pool_y.shape[1]), row(mla_o.shape[1]), row(mem_o.shape[1]), row(3 * d),
                  _resident(w["w_pool_out"].shape), _resident(wmo.shape), _resident(w["w_x_out"].shape),
                  _resident(w["w_o"].shape), _resident((1, d))],
        out_specs=[row(d), pl.BlockSpec((d, tm), lambda i: (0, i))],
        out_shape=[jax.ShapeDtypeStruct((n, d), F32), jax.ShapeDtypeStruct((d, n), BF16)],
        compiler_params=pltpu.CompilerParams(dimension_semantics=("arbitrary",)),
        name="merge",
    )(x, pool_y, mla_o, mem_o, g, w["w_pool_out"], wmo, w["w_x_out"], w["w_o"], w["ffn_norm"])


_ROUTE_T = 512
_NOT_TOP = 99.0
_TOPK_COLS = 2


def _top_k_columns(s, exact_ties):
    nk = s.shape[0]
    rank = jnp.full(s.shape, _NOT_TOP, F32)
    vals = []
    if exact_ties:
        kidx = lax.broadcasted_iota(jnp.int32, s.shape, 0).astype(F32)
    for r in range(PEER_TOPK):
        m = jnp.max(s, axis=0, keepdims=True)
        if exact_ties:
            first = jnp.min(jnp.where(s == m, kidx, float(nk)), axis=0, keepdims=True)
            sel = kidx == first
        else:
            sel = s == m
        rank = jnp.where(sel, float(r), rank)
        s = jnp.where(sel, NEG_INF, s)
        vals.append(m)
    return jnp.concatenate(vals, axis=0), rank, s


def _merge_top_pairs(v1, v2):
    aidx = lax.broadcasted_iota(jnp.int32, v1.shape, 0).astype(F32)
    kf = float(PEER_TOPK)
    taken = jnp.zeros(v1.shape, F32)
    front = v1 + v2[0:1, :]
    top = front[0:1, :]
    z = jnp.zeros_like(top)
    for _ in range(PEER_TOPK):
        m = jnp.max(front, axis=0, keepdims=True)
        a_star = jnp.min(jnp.where(front == m, aidx, kf), axis=0, keepdims=True)
        sel = aidx == a_star
        z = z + jnp.exp(m - top)
        taken = taken + jnp.where(sel, 1.0, 0.0)
        nxt = jnp.sum(jnp.where(sel, taken, 0.0), axis=0, keepdims=True)
        v2n = jnp.sum(jnp.where(aidx == nxt, v2, 0.0), axis=0, keepdims=True)
        v2n = jnp.where(nxt >= kf, NEG_INF, v2n)
        front = jnp.where(sel, v1 + v2n, front)
    return taken, z


def _route_kernel(hnt_ref, wqt_ref, k1_ref, k2_ref, nb1_ref, e1_ref, r2_ref, e2_ref,
                  s_scr, rank_scr, val_scr, exp_scr):
    ncol = s_scr.shape[1]
    qt = _dot(wqt_ref[...], hnt_ref[...]).astype(BF16)
    for h in range(PEER_HEADS):
        q0 = h * 2 * PEER_HALF
        s1 = _dot(k1_ref[...], qt[q0:q0 + PEER_HALF, :])
        s2 = _dot(k2_ref[...], qt[q0 + PEER_HALF:q0 + 2 * PEER_HALF, :])
        for c in range(ncol):
            s_scr[2 * h, c] = s1[:, c * LANES:(c + 1) * LANES]
            s_scr[2 * h + 1, c] = s2[:, c * LANES:(c + 1) * LANES]

    groups = ncol // _TOPK_COLS

    def topk_body(i, carry):
        hh = i // groups
        c0 = (i % groups) * _TOPK_COLS
        miss = jnp.zeros((1, LANES), F32)
        for k in range(_TOPK_COLS):
            s = s_scr[hh, c0 + k]
            vals, rank, rest = _top_k_columns(s, exact_ties=False)
            val_scr[hh, c0 + k] = vals
            rank_scr[hh, c0 + k] = rank
            exp_scr[hh, c0 + k] = jnp.exp(s - vals[0:1, :])
            removed = jnp.sum(jnp.where(rest == NEG_INF, 1.0, 0.0), axis=0, keepdims=True)
            miss = jnp.maximum(miss, jnp.abs(removed - float(PEER_TOPK)))

        @pl.when(jnp.max(miss) > 0.0)
        def _():
            for k in range(_TOPK_COLS):
                vals, rank, _ = _top_k_columns(s_scr[hh, c0 + k], exact_ties=True)
                val_scr[hh, c0 + k] = vals
                rank_scr[hh, c0 + k] = rank

        return carry

    lax.fori_loop(0, 2 * PEER_HEADS * groups, topk_body, 0)

    def merge_body(h, carry):
        for c in range(ncol):
            taken, z = _merge_top_pairs(val_scr[2 * h, c], val_scr[2 * h + 1, c])
            rank1 = rank_scr[2 * h, c]
            nb1 = jnp.zeros(rank1.shape, F32)
            for a in range(PEER_TOPK):
                nb1 = nb1 + jnp.where(rank1 == float(a), taken[a:a + 1, :], 0.0)
            nb1_ref[h, c] = nb1
            e1_ref[h, c] = exp_scr[2 * h, c] * (1.0 / z)
            r2_ref[h, c] = _pack_bf16(rank_scr[2 * h + 1, c])
            e2_ref[h, c] = _pack_bf16(exp_scr[2 * h + 1, c])
        return carry

    lax.fori_loop(0, PEER_HEADS, merge_body, 0)


def _peer_route(hnt, w):
    d, n = hnt.shape
    t = _ROUTE_T
    ncol = t // LANES
    assert ncol % _TOPK_COLS == 0
    blk = lambda rows: pl.BlockSpec((PEER_HEADS, ncol, rows, LANES), lambda i: (0, i, 0, 0))
    shp = lambda rows, dt: jax.ShapeDtypeStruct((PEER_HEADS, n // LANES, rows, LANES), dt)
    return pl.pallas_call(
        _route_kernel,
        grid=(n // t,),
        in_specs=[pl.BlockSpec((d, t), lambda i: (0, i)), _resident(w["w_peer_qt"].shape),
                  _resident((N_KEYS, PEER_HALF)), _resident((N_KEYS, PEER_HALF))],
        out_specs=[blk(N_KEYS), blk(N_KEYS), blk(N_KEYS // 2), blk(N_KEYS // 2)],
        out_shape=[shp(N_KEYS, F32), shp(N_KEYS, F32), shp(N_KEYS // 2, jnp.uint32),
                   shp(N_KEYS // 2, jnp.uint32)],
        scratch_shapes=[pltpu.VMEM((2 * PEER_HEADS, ncol, N_KEYS, LANES), F32),
                        pltpu.VMEM((2 * PEER_HEADS, ncol, N_KEYS, LANES), F32),
                        pltpu.VMEM((2 * PEER_HEADS, ncol, PEER_TOPK, LANES), F32),
                        pltpu.VMEM((2 * PEER_HEADS, ncol, N_KEYS, LANES), F32)],
        compiler_params=pltpu.CompilerParams(dimension_semantics=("arbitrary",)),
        name="peer_route",
    )(hnt, w["w_peer_qt"], w["peer_k1"], w["peer_k2"])


_DENSE_T = 512
_DENSE_E = 1024
_DENSE_MM_E = 256


def _gelu(x):
    return 0.5 * x * (1.0 + lax.erf(x * float(1.0 / np.sqrt(2.0))))


def _dense_kernel(hnt_ref, u_ref, vt_ref, nb1_ref, e1_ref, r2_ref, e2_ref, h_ref, fn_ref, y_ref, acc_ref,
                  st_scr):
    e = pl.program_id(1)
    t = hnt_ref.shape[1]
    n_i1 = u_ref.shape[0] // N_KEYS
    i1_per_mm = _DENSE_MM_E // N_KEYS

    @pl.when(e == 0)
    def _():
        acc_ref[...] = jnp.zeros(acc_ref.shape, F32)

    zero = jnp.zeros((), BF16)
    tile = (N_KEYS, LANES)
    n_chunks = n_i1 // i1_per_mm
    chunk = lambda ec: slice(ec * _DENSE_MM_E, (ec + 1) * _DENSE_MM_E)
    st_scr[0] = _dot(u_ref[chunk(0), :], hnt_ref[...])
    for ec in range(n_chunks):
        esl = chunk(ec)
        if ec + 1 < n_chunks:
            st_scr[(ec + 1) % 2] = _dot(u_ref[chunk(ec + 1), :], hnt_ref[...])
        p_rows = []
        for j2 in range(i1_per_mm):
            j = ec * i1_per_mm + j2
            a = _gelu(st_scr[ec % 2, j2 * N_KEYS:(j2 + 1) * N_KEYS, :]).astype(BF16)
            parts = []
            for c in range(t // LANES):
                g = jnp.zeros(tile, BF16)
                for h in range(PEER_HEADS):
                    nb = jnp.broadcast_to(nb1_ref[h, c, j:j + 1, :], tile).astype(BF16)
                    e1 = jnp.broadcast_to(e1_ref[h, c, j:j + 1, :], tile).astype(BF16)
                    r2 = _unpack_bf16(r2_ref[h, c])
                    e2 = _unpack_bf16(e2_ref[h, c])
                    g = g + jnp.where(r2 < nb, e2 * e1, zero)
                parts.append(g)
            p_rows.append(jnp.concatenate(parts, axis=1) * a)
        pt = jnp.concatenate(p_rows, axis=0)
        acc_ref[...] += _dot(vt_ref[:, esl], pt)

    @pl.when(e == pl.num_programs(1) - 1)
    def _():
        y_ref[...] = _rms(h_ref[...] + acc_ref[...].T, fn_ref[...])


def _peer_dense(hnt, h, route, w):
    d, n = hnt.shape
    t = _DENSE_T
    ne = w["peer_u"].shape[0]
    ncol = t // LANES
    n_i1 = _DENSE_E // N_KEYS
    nb1, e1, r2, e2 = route
    row_blk = pl.BlockSpec((PEER_HEADS, ncol, n_i1, LANES), lambda i, e: (0, i, e, 0))
    full_blk = pl.BlockSpec((PEER_HEADS, ncol) + r2.shape[2:], lambda i, e: (0, i, 0, 0))
    return pl.pallas_call(
        _dense_kernel,
        grid=(n // t, ne // _DENSE_E),
        in_specs=[pl.BlockSpec((d, t), lambda i, e: (0, i)),
                  pl.BlockSpec((_DENSE_E, d), lambda i, e: (e, 0)),
                  pl.BlockSpec((d, _DENSE_E), lambda i, e: (0, e)),
                  row_blk, row_blk, full_blk, full_blk,
                  pl.BlockSpec((t, d), lambda i, e: (i, 0)),
                  pl.BlockSpec((1, d), lambda i, e: (0, 0))],
        out_specs=pl.BlockSpec((t, d), lambda i, e: (i, 0)),
        out_shape=jax.ShapeDtypeStruct((n, d), F32),
        scratch_shapes=[pltpu.VMEM((d, t), F32), pltpu.VMEM((2, _DENSE_MM_E, t), F32)],
        compiler_params=pltpu.CompilerParams(dimension_semantics=("arbitrary", "arbitrary")),
        name="peer_dense",
    )(hnt, w["peer_u"], w["peer_vt"], nb1, e1, r2, e2, h, w["final_norm"])


def _rotate_half_cols(w):
    half = QK_ROPE // 2
    return jnp.concatenate([-w[..., half:], w[..., :half]], axis=-1)


def _prep_weights(attn_norm, w_in, q_norm, kv_norm, w_q_up, w_uk, w_uv, w_pool_grp, pool_scale, mem_norm,
                  w_mem_k, w_mem_v, w_pool_out, w_mla_out, w_x_out, w_o, ffn_norm, w_peer_q, peer_k1,
                  peer_k2, peer_u, peer_v, final_norm):
    d = w_in.shape[0]
    pool_w = POOL_GROUP * len(POOL_WINDOWS)
    xw = X_HEADS * X_HEAD_DIM
    o = np.cumsum([0, pool_w, Q_LORA, KV_LORA, QK_ROPE, xw, 3 * d])
    seg = [w_in[:, o[i]:o[i + 1]] for i in range(6)]
    kr = seg[3]

    def rope_slot(cols):
        z = jnp.zeros(cols.shape[:-1] + (QK_NOPE,), F32)
        z2 = jnp.zeros(cols.shape[:-1] + (HEAD_PAD - QK_NOPE - QK_ROPE,), F32)
        return jnp.concatenate([z, cols, z2], axis=-1)

    w_in_p = jnp.concatenate([seg[0], seg[1], seg[2], rope_slot(kr), rope_slot(_rotate_half_cols(kr)),
                              seg[4], seg[5]], axis=1).astype(BF16)
    q_nope = w_q_up[:, :, :QK_NOPE]
    q_rope = w_q_up[:, :, QK_NOPE:]
    tail = jnp.zeros(q_rope.shape[:2] + (HEAD_PAD - QK_NOPE - QK_ROPE,), F32)
    qa = jnp.concatenate([q_nope, q_rope, tail], axis=-1).reshape(Q_LORA, MLA_HEADS * HEAD_PAD)
    qb = rope_slot(_rotate_half_cols(q_rope)).reshape(Q_LORA, MLA_HEADS * HEAD_PAD)
    w_q = jnp.concatenate([qa, qb], axis=1).astype(BF16)
    padk = jnp.zeros((KV_LORA, MLA_HEADS, HEAD_PAD - QK_NOPE), F32)
    padv = jnp.zeros((KV_LORA, MLA_HEADS, HEAD_PAD - V_HEAD), F32)
    w_kv = jnp.concatenate([jnp.concatenate([w_uk, padk], -1).reshape(KV_LORA, -1),
                            jnp.concatenate([w_uv, padv], -1).reshape(KV_LORA, -1)], axis=1).astype(BF16)
    ukt = jnp.transpose(w_uk, (1, 2, 0))
    top = jnp.concatenate([ukt, jnp.zeros((MLA_HEADS, QK_NOPE, LANES), F32)], axis=-1)
    mid = jnp.concatenate([jnp.zeros((QK_ROPE, KV_LORA), F32), jnp.eye(QK_ROPE, dtype=F32),
                           jnp.zeros((QK_ROPE, LANES - QK_ROPE), F32)], axis=-1)
    mid = jnp.broadcast_to(mid[None], (MLA_HEADS,) + mid.shape)
    bot = jnp.zeros((MLA_HEADS, HEAD_PAD - QK_NOPE - QK_ROPE, KV_LORA + LANES), F32)
    w_abs = jnp.concatenate([top, mid, bot], axis=1).astype(BF16)
    w_mla_pad = jnp.concatenate([w_mla_out.reshape(MLA_HEADS, V_HEAD, d),
                                 jnp.zeros((MLA_HEADS, HEAD_PAD - V_HEAD, d), F32)], axis=1)
    return dict(
        attn_norm=attn_norm.reshape(1, d), w_in=w_in_p, q_norm=q_norm.reshape(1, -1),
        kv_norm=kv_norm.reshape(1, -1), w_q=w_q, w_kv=w_kv, w_abs=w_abs,
        w_uv_flat=w_uv.reshape(KV_LORA, MLA_HEADS * V_HEAD).astype(BF16),
        w_pool_grp=w_pool_grp.astype(BF16), pool_scale=pool_scale.reshape(1, -1),
        mem_norm=mem_norm.reshape(1, d),
        w_mem_kv=jnp.concatenate([w_mem_k.reshape(d, xw), w_mem_v.reshape(d, xw)], axis=1).astype(BF16),
        w_pool_out=w_pool_out.astype(BF16), w_mla_out=w_mla_out.astype(BF16),
        w_mla_out_pad=w_mla_pad.reshape(MLA_HEADS * HEAD_PAD, d).astype(BF16),
        w_x_out=w_x_out.astype(BF16), w_o=w_o.astype(BF16), ffn_norm=ffn_norm.reshape(1, d),
        w_peer_qt=w_peer_q.reshape(d, -1).T.astype(BF16), peer_k1=peer_k1.astype(BF16),
        peer_k2=peer_k2.astype(BF16), peer_u=peer_u.astype(BF16), peer_vt=peer_v.T.astype(BF16),
        final_norm=final_norm.reshape(1, d),
    )


def _rope_tables(start, t):
    half = QK_ROPE // 2
    inv = jnp.power(ROPE_BASE, -jnp.arange(half, dtype=F32) / half)
    pos = (start + jnp.arange(t, dtype=jnp.int32)).astype(F32)
    ang = pos[:, None] * inv[None, :]
    cos, sin = jnp.cos(ang), jnp.sin(ang)
    one = jnp.ones((t, QK_NOPE), F32)
    z_nope = jnp.zeros((t, QK_NOPE), F32)
    z_tail = jnp.zeros((t, HEAD_PAD - QK_NOPE - QK_ROPE), F32)
    return (jnp.concatenate([one, cos, cos, z_tail], axis=1),
            jnp.concatenate([z_nope, sin, sin, z_tail], axis=1))


def _peer_and_norm(h, hnt, w):
    return _peer_dense(hnt, h, _peer_route(hnt, w), w)


def kernel(x_prompt, x_sample, cache_kv_latent, cache_k_rope, state_pool, cache_mem_k, cache_mem_v, page_table, mem_prompt, attn_norm, w_in, q_norm, kv_norm, w_q_up, w_uk, w_uv, w_pool_grp, pool_scale, mem_norm, w_mem_k, w_mem_v, w_pool_out, w_mla_out, w_x_out, w_o, ffn_norm, w_peer_q, peer_k1, peer_k2, peer_u, peer_v, final_norm):
    depth = attn_norm.shape[0]
    assert depth == 1, "single trunk layer"
    b, s, d = x_prompt.shape
    bd, td, _ = x_sample.shape
    n_pages = page_table.shape[1]
    past = n_pages * cache_kv_latent.shape[2]
    xw = X_HEADS * X_HEAD_DIM
    w = _prep_weights(attn_norm[0], w_in[0], q_norm[0], kv_norm[0], w_q_up[0], w_uk[0], w_uv[0],
                      w_pool_grp[0], pool_scale[0], mem_norm[0], w_mem_k[0], w_mem_v[0], w_pool_out[0],
                      w_mla_out[0], w_x_out[0], w_o[0], ffn_norm[0], w_peer_q[0], peer_k1[0], peer_k2[0],
                      peer_u[0], peer_v[0], final_norm)
    rope_lanes = slice(QK_NOPE, QK_NOPE + QK_ROPE)

    xp = x_prompt.reshape(b * s, d)
    cos_p, sin_p = _rope_tables(0, s)
    tm = min(512, b * s)
    u, c_kv, kr, qx, g, q_att, k_att, v_att = _proj_in(xp, cos_p, sin_p, s // tm, w, sample=False)
    u3 = u.reshape(b, s, -1)
    pool_y = _pool_mix(u3, jnp.zeros((b, POOL_HIST, u3.shape[-1]), F32), 0, w)
    mla_o = _prompt_attention(q_att.reshape(b, s, -1), k_att.reshape(b, s, -1), v_att.reshape(b, s, -1))
    n_mem = mem_prompt.shape[1]
    mem_k, mem_v = _memory_kv(mem_prompt.reshape(b * n_mem, d), w)
    mem_o = _memory_attention(qx.reshape(b, s, xw), mem_k.reshape(b, n_mem, xw), mem_v.reshape(b, n_mem, xw))
    h, hnt = _merge(xp, pool_y.reshape(b * s, -1), mla_o.reshape(b * s, -1), mem_o.reshape(b * s, xw), g, w,
                    w["w_mla_out_pad"])
    y_prompt = _peer_and_norm(h, hnt, w).reshape(b, s, d)

    xs = x_sample.reshape(bd * td, d)
    cos_s, sin_s = _rope_tables(past, td)
    reps = (bd * td) // td
    cos_s = jnp.tile(cos_s, (reps, 1))
    sin_s = jnp.tile(sin_s, (reps, 1))
    us, c_kv_s, kr_s, qx_s, g_s, q_s = _proj_in(xs, cos_s, sin_s, 1, w, sample=True)
    us3 = us.reshape(bd, td, -1)
    pool_y_s = _pool_mix(us3, state_pool[0], past, w)
    t8 = 8
    q_rows = q_s.reshape(bd, td, MLA_HEADS, 384).transpose(0, 2, 1, 3)
    q_rows = jnp.pad(q_rows, ((0, 0), (0, 0), (0, t8 - td), (0, 0))).reshape(bd, MLA_HEADS * t8, 384)
    ckv8 = jnp.pad(c_kv_s.reshape(bd, td, KV_LORA), ((0, 0), (0, t8 - td), (0, 0)))
    kr_new = kr_s[:, rope_lanes].reshape(bd, td, QK_ROPE)
    kr8 = jnp.pad(kr_new, ((0, 0), (0, t8 - td), (0, 0)))
    kr_pages_t = jnp.swapaxes(cache_k_rope[0], 1, 2)
    mla_o_s = _sample_attention(q_rows, ckv8, kr8, page_table, cache_kv_latent[0], kr_pages_t, w)
    mla_o_s = mla_o_s[:, :td].reshape(bd * td, -1)
    mem_o_s = _memory_attention_rows(qx_s.reshape(bd, td * X_HEADS, X_HEAD_DIM),
                                     cache_mem_k[0].reshape(bd, -1, X_HEAD_DIM),
                                     cache_mem_v[0].reshape(bd, -1, X_HEAD_DIM))
    h_s, hnt_s = _merge(xs, pool_y_s.reshape(bd * td, -1), mla_o_s, mem_o_s.reshape(bd * td, xw), g_s, w,
                        w["w_mla_out"])
    y_sample = _peer_and_norm(h_s, hnt_s, w).reshape(bd, td, d)

    new_pool_s = jnp.concatenate([state_pool[0], us3], axis=1)[:, -POOL_HIST:]
    return (y_prompt, y_sample,
            c_kv.reshape(1, b, s, KV_LORA), kr[:, rope_lanes].reshape(1, b, s, QK_ROPE),
            u3[:, -POOL_HIST:][None],
            mem_k.reshape(1, b, n_mem, X_HEADS, X_HEAD_DIM), mem_v.reshape(1, b, n_mem, X_HEADS, X_HEAD_DIM),
            c_kv_s.reshape(1, bd, td, KV_LORA), kr_new[None], new_pool_s[None])
```
